```python
import jax, jax.numpy as jnp
from jax import lax
import numpy as np

D_MODEL = 1024
BATCH = 8
SEQ = 8192
DEPTH = 4

CHUNK = 64
N_EVEN = (DEPTH + 1) // 2
N_ODD = DEPTH // 2
D_FF = 4 * D_MODEL
RMS_EPS = 1e-6
RET_WIDTH = D_MODEL // 2
RET_HEADS = 4
RET_HEAD_DIM = RET_WIDTH // RET_HEADS
RET_ROPE_BASE = 10000.0
GN_EPS = 1e-5
POOL_WIDTH = D_MODEL - RET_WIDTH
POOL_WINDOWS = (2, 4, 8, 16)
POOL_GROUPS = len(POOL_WINDOWS)
POOL_GROUP_DIM = POOL_WIDTH // POOL_GROUPS
AB_IN_WIDTH = 4 * RET_WIDTH + POOL_WIDTH
AB_OUT_WIDTH = RET_WIDTH + POOL_WIDTH
ATT_HEADS = 16
ATT_HEAD_DIM = D_MODEL // ATT_HEADS
LEFT_CHUNKS = 8
BAND = (LEFT_CHUNKS + 1) * CHUNK
REL_CLIP = 128
N_REL = 2 * REL_CLIP + 1
NEG_INF = -1e30

kernel_name = "hybrid_retention_pool_chunkattn_trunk"


def rms_norm(x, g):
    xf = x.astype(jnp.float32)
    y = xf * lax.rsqrt(jnp.mean(xf * xf, axis=-1, keepdims=True) + RMS_EPS)
    return (y * g.astype(jnp.float32)).astype(x.dtype)


def rotary(x):
    S, d = x.shape[1], x.shape[-1]
    inv_freq = 1.0 / (RET_ROPE_BASE ** jnp.linspace(0.0, 1.0, d // 2, dtype=jnp.float32))
    ang = jnp.arange(S, dtype=jnp.float32)[:, None] * inv_freq[None, :]
    cos = jnp.cos(ang)[None, :, None, :]
    sin = jnp.sin(ang)[None, :, None, :]
    x1, x2 = x[..., 0::2], x[..., 1::2]
    return jnp.stack([x1 * cos - x2 * sin, x1 * sin + x2 * cos], axis=-1).reshape(x.shape)


def retention(q, k, v):
    B, S, H, d = q.shape
    nc = S // CHUNK
    log_g = jnp.log1p(-jnp.power(2.0, -5.0 - jnp.arange(H, dtype=jnp.float32)))
    pos = jnp.arange(CHUNK, dtype=jnp.float32)
    intra_decay = jnp.exp(jnp.abs(pos[:, None] - pos[None, :])[None] * log_g[:, None, None])
    q_decay = jnp.exp((pos[:, None] + 1.0) * log_g[None, :])
    k_decay = jnp.exp((CHUNK - 1.0 - pos[:, None]) * log_g[None, :])
    chunk_decay = jnp.exp(CHUNK * log_g)

    qc = q.reshape(B, nc, CHUNK, H, d)
    kc = k.reshape(B, nc, CHUNK, H, d)
    vc = v.reshape(B, nc, CHUNK, H, d)
    scores = jnp.einsum('bcnhd,bcmhd->bchnm', qc, kc) * intra_decay
    intra = jnp.einsum('bchnm,bcmhe->bcnhe', scores, vc)

    def step(state, inp):
        q_i, k_i, v_i = inp
        cross = jnp.einsum('bnhd,bhde->bnhe', q_i * q_decay[:, :, None], state)
        state = state * chunk_decay[:, None, None] + jnp.einsum(
            'bmhd,bmhe->bhde', k_i * k_decay[:, :, None], v_i)
        return state, cross

    xs = (jnp.moveaxis(qc, 1, 0), jnp.moveaxis(kc, 1, 0), jnp.moveaxis(vc, 1, 0))
    state0 = jnp.zeros((B, H, d, d), jnp.float32)
    _, cross = lax.scan(step, state0, xs)
    out = intra + jnp.moveaxis(cross, 0, 1)
    return out.reshape(B, S, H, d)


def head_group_norm(o, g):
    B, S, H, d = o.shape
    mu = jnp.mean(o, axis=-1, keepdims=True)
    var = jnp.mean(jnp.square(o - mu), axis=-1, keepdims=True)
    y = (o - mu) * lax.rsqrt(var + GN_EPS)
    return y.reshape(B, S, H * d) * g


def multiscale_pool(p, w_group, scale):
    B, S, _ = p.shape
    pf = p.astype(jnp.float32).reshape(B, S, POOL_GROUPS, POOL_GROUP_DIM)
    csum = lax.cumsum(pf, axis=1)
    t = jnp.arange(S)
    outs = []
    for gi, w in enumerate(POOL_WINDOWS):
        c = csum[:, :, gi]
        lagged = jnp.pad(c, ((0, 0), (w, 0), (0, 0)))[:, :S]
        count = jnp.minimum(t + 1, w).astype(jnp.float32)[None, :, None]
        outs.append((c - lagged) / count - pf[:, :, gi])
    pooled = jnp.stack(outs, axis=2)
    mixed = jnp.einsum('bsgc,gce->bsge', pooled, w_group.astype(jnp.float32))
    return (mixed.reshape(B, S, POOL_WIDTH) * scale.astype(jnp.float32)).astype(p.dtype)


def retention_pool_mixer(h, w_in, gn_gain, w_pool, pool_scale, w_out):
    B, S, _ = h.shape
    z = h @ w_in
    q, k, v, g, p = jnp.split(
        z, [RET_WIDTH, 2 * RET_WIDTH, 3 * RET_WIDTH, 4 * RET_WIDTH], axis=-1)

    def heads(t):
        return t.astype(jnp.float32).reshape(B, S, RET_HEADS, RET_HEAD_DIM)

    qh = rotary(heads(q))
    kh = rotary(heads(k)) * (RET_HEAD_DIM ** -0.5)
    o = retention(qh, kh, heads(v))
    o = head_group_norm(o, gn_gain.astype(jnp.float32))
    ret_out = (jax.nn.silu(g.astype(jnp.float32)) * o).astype(h.dtype)
    pool_out = multiscale_pool(p, w_pool, pool_scale)
    return jnp.concatenate([ret_out, pool_out], axis=-1) @ w_out


def chunk_attention(h, w_qkv, rel_bias, w_out):
    B, S, _ = h.shape
    nc = S // CHUNK
    q, k, v = jnp.split(h @ w_qkv, 3, axis=-1)
    q = q.reshape(B, S, ATT_HEADS, ATT_HEAD_DIM) * (ATT_HEAD_DIM ** -0.5)
    pad = ((0, 0), (LEFT_CHUNKS * CHUNK, 0), (0, 0), (0, 0))
    k = jnp.pad(k.reshape(B, S, ATT_HEADS, ATT_HEAD_DIM), pad)
    v = jnp.pad(v.reshape(B, S, ATT_HEADS, ATT_HEAD_DIM), pad)
    n_idx = jnp.arange(CHUNK)[:, None]
    j_idx = jnp.arange(BAND)[None, :]
    rel = jnp.clip(n_idx + LEFT_CHUNKS * CHUNK - j_idx, -REL_CLIP, REL_CLIP) + REL_CLIP
    bias = rel_bias.astype(jnp.float32)[:, rel]
    band_pos = jnp.arange(BAND)

    def one_chunk(c):
        q_c = lax.dynamic_slice_in_dim(q, c * CHUNK, CHUNK, axis=1)
        k_b = lax.dynamic_slice_in_dim(k, c * CHUNK, BAND, axis=1)
        v_b = lax.dynamic_slice_in_dim(v, c * CHUNK, BAND, axis=1)
        s = jnp.einsum('bnhd,bjhd->bhnj', q_c, k_b).astype(jnp.float32) + bias
        valid = band_pos >= (LEFT_CHUNKS - c) * CHUNK
        s = jnp.where(valid, s, NEG_INF)
        pr = jax.nn.softmax(s, axis=-1).astype(v_b.dtype)
        return jnp.einsum('bhnj,bjhd->bnhd', pr, v_b)

    o = lax.map(one_chunk, jnp.arange(nc))
    o = jnp.moveaxis(o, 0, 1).reshape(B, S, D_MODEL)
    return o @ w_out


def squared_relu_mlp(h, w1, w2):
    return jnp.square(jax.nn.relu(h @ w1)) @ w2


def setup_inputs(seed: int = 0) -> dict:
    key = jax.random.key(seed)
    ks = jax.random.split(key, 16)
    f32 = jnp.float32

    def nrm(k, shape, scale):
        return jax.random.normal(k, shape, f32) * scale

    return {
        "x": nrm(ks[0], (BATCH, SEQ, D_MODEL), 1.0),
        "mix_norm": 1.0 + nrm(ks[1], (DEPTH, D_MODEL), 0.05),
        "ffn_norm": 1.0 + nrm(ks[2], (DEPTH, D_MODEL), 0.05),
        "w_ffn_in": nrm(ks[3], (DEPTH, D_MODEL, D_FF), D_MODEL ** -0.5),
        "w_ffn_out": nrm(ks[4], (DEPTH, D_FF, D_MODEL), D_FF ** -0.5),
        "ab_w_in": nrm(ks[5], (N_EVEN, D_MODEL, AB_IN_WIDTH), D_MODEL ** -0.5),
        "ab_gn_gain": 1.0 + nrm(ks[6], (N_EVEN, RET_WIDTH), 0.05),
        "ab_w_pool": nrm(ks[7], (N_EVEN, POOL_GROUPS, POOL_GROUP_DIM, POOL_GROUP_DIM), POOL_GROUP_DIM ** -0.5),
        "ab_pool_scale": 1.0 + nrm(ks[8], (N_EVEN, POOL_WIDTH), 0.1),
        "ab_w_out": nrm(ks[9], (N_EVEN, AB_OUT_WIDTH, D_MODEL), AB_OUT_WIDTH ** -0.5),
        "c_w_qkv": nrm(ks[10], (N_ODD, D_MODEL, 3 * D_MODEL), D_MODEL ** -0.5),
        "c_rel_bias": nrm(ks[11], (N_ODD, ATT_HEADS, N_REL), 0.5),
        "c_w_out": nrm(ks[12], (N_ODD, D_MODEL, D_MODEL), D_MODEL ** -0.5),
        "final_norm": 1.0 + nrm(ks[13], (D_MODEL,), 0.05),
    }


def reference(x, mix_norm, ffn_norm, w_ffn_in, w_ffn_out, ab_w_in, ab_gn_gain,
              ab_w_pool, ab_pool_scale, ab_w_out, c_w_qkv, c_rel_bias, c_w_out,
              final_norm):
    for layer in range(DEPTH):
        h = rms_norm(x, mix_norm[layer])
        i = layer // 2
        if layer % 2 == 0:
            x = x + retention_pool_mixer(h, ab_w_in[i], ab_gn_gain[i], ab_w_pool[i],
                                         ab_pool_scale[i], ab_w_out[i])
        else:
            x = x + chunk_attention(h, c_w_qkv[i], c_rel_bias[i], c_w_out[i])
        x = x + squared_relu_mlp(rms_norm(x, ffn_norm[layer]), w_ffn_in[layer], w_ffn_out[layer])
    return rms_norm(x, final_norm)
```

```python
import functools
import math

import jax
import jax.numpy as jnp
from jax import lax
from jax.experimental import pallas as pl
from jax.experimental.pallas import tpu as pltpu

F32 = jnp.float32
BF16 = jnp.bfloat16

D_MODEL = 1024
D_FF = 4 * D_MODEL
CHUNK = 64
RMS_EPS = 1e-6
RET_WIDTH = 512
RET_HEADS = 4
RET_HEAD_DIM = 128
RET_ROPE_BASE = 10000.0
GN_EPS = 1e-5
POOL_WIDTH = 512
POOL_WINDOWS = (2, 4, 8, 16)
POOL_GROUP_DIM = 128
MAX_WINDOW = max(POOL_WINDOWS)
AB_IN_WIDTH = 4 * RET_WIDTH + POOL_WIDTH
ATT_HEADS = 16
ATT_HEAD_DIM = 64
LEFT_CHUNKS = 8
BAND = (LEFT_CHUNKS + 1) * CHUNK
REL_CLIP = 128
N_REL = 2 * REL_CLIP + 1
NEG_INF = -1e30

V7X_LANES = 128
V7X_VMEM_BYTES = 64 * 1024 * 1024
V7X_VMEM_HEADROOM_BYTES = 8 * 1024 * 1024

TOKEN_TILE = 512
RET_TILE = 256
FFN_HIDDEN_TILE = 1024
ATT_GROUP_CHUNKS = 2
ATT_GROUP_ROWS = ATT_GROUP_CHUNKS * CHUNK
ATT_GROUP_BAND = BAND + (ATT_GROUP_CHUNKS - 1) * CHUNK
REL_PAD = 3 * V7X_LANES
ROT_TABLE_TILE = 1024


def _vmem_limit(block_bytes):
    return int(min(block_bytes + 24 * 1024 * 1024, V7X_VMEM_BYTES - V7X_VMEM_HEADROOM_BYTES))


def _params(n_axes, block_bytes):
    return pltpu.CompilerParams(
        dimension_semantics=("arbitrary",) * n_axes,
        vmem_limit_bytes=_vmem_limit(block_bytes),
    )


def _resident(shape):
    nd = len(shape)
    return pl.BlockSpec(shape, lambda *_: (0,) * nd, pipeline_mode=pl.Buffered(1))


def _rms(x, g):
    return x * lax.rsqrt(jnp.mean(x * x, axis=-1, keepdims=True) + RMS_EPS) * g


def _dot(a, b):
    return jnp.dot(a, b, preferred_element_type=F32)


def _dot_nt(a, b):
    return lax.dot_general(a, b, (((1,), (1,)), ((), ())), preferred_element_type=F32)


def _dot_tn(a, b):
    return lax.dot_general(a, b, (((0,), (0,)), ((), ())), preferred_element_type=F32)


def _rotary_table_kernel(cos_ref, sin_ref):
    rows = cos_ref.shape[0]
    n = (lax.broadcasted_iota(jnp.int32, (rows, RET_HEAD_DIM), 0) + pl.program_id(0) * rows).astype(F32)
    lane = lax.broadcasted_iota(jnp.int32, (rows, RET_HEAD_DIM), 1)
    frac = (lane >> 1).astype(F32) / float(RET_HEAD_DIM // 2 - 1)
    inv_freq = 1.0 / jnp.exp(frac * math.log(RET_ROPE_BASE))
    ang = n * inv_freq
    cos_ref[...] = jnp.cos(ang)
    sin_ref[...] = jnp.where((lane & 1) == 0, -1.0, 1.0) * jnp.sin(ang)


def _rotary_tables(seq):
    tile = min(ROT_TABLE_TILE, seq)
    shape = jax.ShapeDtypeStruct((seq, RET_HEAD_DIM), F32)
    spec = pl.BlockSpec((tile, RET_HEAD_DIM), lambda i: (i, 0))
    return pl.pallas_call(
        _rotary_table_kernel,
        grid=(seq // tile,),
        out_specs=(spec, spec),
        out_shape=(shape, shape),
        compiler_params=_params(1, 0),
        name="rotary_tables",
    )()


def _ret_log_gamma(h):
    return math.log1p(-(2.0 ** (-5.0 - h)))


def _decay_table_kernel(dmask_ref, qdec_ref, kdec_ref):
    t = lax.broadcasted_iota(jnp.int32, (RET_TILE, RET_TILE), 0)
    m = lax.broadcasted_iota(jnp.int32, (RET_TILE, RET_TILE), 1)
    dist = jnp.abs(t - m).astype(F32)
    chunk_shift = CHUNK.bit_length() - 1
    visible = (m >> chunk_shift) <= (t >> chunk_shift)
    tq = lax.broadcasted_iota(jnp.int32, (RET_TILE, RET_HEAD_DIM), 0).astype(F32)
    for h in range(RET_HEADS):
        lg = _ret_log_gamma(h)
        dmask_ref[h] = jnp.where(visible, jnp.exp(dist * lg), 0.0)
        qdec_ref[h] = jnp.exp((tq + 1.0) * lg)
        kdec_ref[h] = jnp.exp((RET_TILE - 1.0 - tq) * lg)


def _decay_tables():
    return pl.pallas_call(
        _decay_table_kernel,
        out_shape=(
            jax.ShapeDtypeStruct((RET_HEADS, RET_TILE, RET_TILE), F32),
            jax.ShapeDtypeStruct((RET_HEADS, RET_TILE, RET_HEAD_DIM), F32),
            jax.ShapeDtypeStruct((RET_HEADS, RET_TILE, RET_HEAD_DIM), F32),
        ),
        name="decay_tables",
    )()


_BIAS_ROW_TILE = 8
_BIAS_LEFT_PAD = CHUNK
_BIAS_EXT = 7 * V7X_LANES
_BIAS_WIDE = 6 * V7X_LANES


def _split3(x):
    hi = x.astype(BF16)
    r1 = x - hi.astype(F32)
    mid = r1.astype(BF16)
    lo = (r1 - mid.astype(F32)).astype(BF16)
    return hi, mid, lo


def _bias_table_kernel(rel_ref, out_ref):
    r = lax.broadcasted_iota(jnp.int32, (REL_PAD, _BIAS_EXT), 0)
    u = lax.broadcasted_iota(jnp.int32, (REL_PAD, _BIAS_EXT), 1)
    d = u - _BIAS_LEFT_PAD
    rel_idx = jnp.minimum(LEFT_CHUNKS * CHUNK + (CHUNK - 1) - d, REL_CLIP) + REL_CLIP
    inside = (d >= 0) & (d < BAND + CHUNK - 1)
    sel = jnp.where(inside & (r == rel_idx), 1.0, 0.0).astype(BF16)
    hi, mid, lo = _split3(rel_ref[...])
    diag = _dot(hi, sel) + _dot(mid, sel) + _dot(lo, sel)

    sub = lax.broadcasted_iota(jnp.int32, (_BIAS_ROW_TILE, _BIAS_WIDE), 0)
    col = lax.broadcasted_iota(jnp.int32, (_BIAS_ROW_TILE, ATT_GROUP_BAND), 1)
    for h in range(ATT_HEADS):
        row = diag[h:h + 1, :]
        skew = jnp.zeros((_BIAS_ROW_TILE, _BIAS_WIDE), F32)
        for b in range(_BIAS_ROW_TILE):
            start = _BIAS_ROW_TILE - 1 - b
            piece = jnp.broadcast_to(row[:, start:start + _BIAS_WIDE], (_BIAS_ROW_TILE, _BIAS_WIDE))
            skew = jnp.where(sub == b, piece, skew)
        for c in range(ATT_GROUP_CHUNKS):
            visible = (col >= c * CHUNK) & (col < c * CHUNK + BAND)
            for a in range(CHUNK // _BIAS_ROW_TILE):
                start = _BIAS_LEFT_PAD + (CHUNK - _BIAS_ROW_TILE) - c * CHUNK - _BIAS_ROW_TILE * a
                tile = skew[:, start:start + ATT_GROUP_BAND]
                r0 = c * CHUNK + _BIAS_ROW_TILE * a
                out_ref[h, r0:r0 + _BIAS_ROW_TILE, :] = jnp.where(visible, tile, NEG_INF)


def _bias_table(rel_bias):
    rel = jnp.pad(rel_bias, ((0, 0), (0, REL_PAD - N_REL)))
    return pl.pallas_call(
        _bias_table_kernel,
        out_shape=jax.ShapeDtypeStruct((ATT_HEADS, ATT_GROUP_ROWS, ATT_GROUP_BAND), F32),
        name="bias_table",
    )(rel)


def _ffn_kernel(x_ref, g_ref, w1_ref, w2_ref, fg_ref, o_ref, *, final_norm):
    x = x_ref[...]
    h = _rms(x, g_ref[...]).astype(BF16)
    acc = x
    for j in range(D_FF // FFN_HIDDEN_TILE):
        cols = slice(j * FFN_HIDDEN_TILE, (j + 1) * FFN_HIDDEN_TILE)
        a = jnp.maximum(_dot(h, w1_ref[:, cols]), 0.0)
        acc = acc + _dot((a * a).astype(BF16), w2_ref[cols, :])
    if final_norm:
        acc = _rms(acc, fg_ref[...])
    o_ref[...] = acc


def _ffn(x2, g, w1, w2, fg, final_norm):
    n = x2.shape[0]
    tile = pl.BlockSpec((TOKEN_TILE, D_MODEL), lambda i: (i, 0))
    vec = _resident((1, D_MODEL))
    block_bytes = 4 * TOKEN_TILE * D_MODEL * 4 + 2 * D_MODEL * D_FF * 2
    return pl.pallas_call(
        functools.partial(_ffn_kernel, final_norm=final_norm),
        grid=(n // TOKEN_TILE,),
        in_specs=[tile, vec, _resident((D_MODEL, D_FF)), _resident((D_FF, D_MODEL)), vec],
        out_specs=tile,
        out_shape=jax.ShapeDtypeStruct((n, D_MODEL), F32),
        compiler_params=_params(1, block_bytes),
        name="ffn",
    )(x2, g, w1, w2, fg)


def _swap_pairs(x):
    lane = lax.broadcasted_iota(jnp.int32, x.shape, 1)
    return jnp.where((lane & 1) == 0, pltpu.roll(x, V7X_LANES - 1, 1), pltpu.roll(x, 1, 1))


def _mixer_kernel(x_ref, g_ref, win_ref, cos_ref, sin_ref, dmask_ref, qdec_ref, kdec_ref,
                  gn_ref, wpool_ref, pscale_ref, wout_ref, o_ref,
                  z_scr, cat_scr, state_scr, pext_scr):
    seq_tile = pl.program_id(1)

    @pl.when(seq_tile == 0)
    def _():
        state_scr[...] = jnp.zeros_like(state_scr)
        pext_scr[0:MAX_WINDOW, :] = jnp.zeros((MAX_WINDOW, POOL_WIDTH), F32)

    x = x_ref[0]
    h = _rms(x, g_ref[...]).astype(BF16)
    z_scr[...] = _dot(h, win_ref[...])

    for hd in range(RET_HEADS):
        lanes = slice(hd * RET_HEAD_DIM, (hd + 1) * RET_HEAD_DIM)
        state_decay = math.exp(RET_TILE * _ret_log_gamma(hd))
        for st in range(TOKEN_TILE // RET_TILE):
            rows = slice(st * RET_TILE, (st + 1) * RET_TILE)
            cos = cos_ref[rows, :]
            sin = sin_ref[rows, :]
            q = z_scr[rows, lanes]
            k = z_scr[rows, RET_WIDTH + hd * RET_HEAD_DIM:RET_WIDTH + (hd + 1) * RET_HEAD_DIM]
            v = z_scr[rows, 2 * RET_WIDTH + hd * RET_HEAD_DIM:2 * RET_WIDTH + (hd + 1) * RET_HEAD_DIM]
            gate = z_scr[rows, 3 * RET_WIDTH + hd * RET_HEAD_DIM:3 * RET_WIDTH + (hd + 1) * RET_HEAD_DIM]
            q = q * cos + _swap_pairs(q) * sin
            k = (k * cos + _swap_pairs(k) * sin) * (RET_HEAD_DIM ** -0.5)
            vb = v.astype(BF16)
            scores = _dot_nt(q.astype(BF16), k.astype(BF16)) * dmask_ref[hd]
            state = state_scr[hd]
            o = _dot(scores.astype(BF16), vb) + _dot((q * qdec_ref[hd]).astype(BF16), state.astype(BF16))
            state_scr[hd] = state * state_decay + _dot_tn((k * kdec_ref[hd]).astype(BF16), vb)
            mu = jnp.mean(o, axis=-1, keepdims=True)
            dev = o - mu
            var = jnp.mean(dev * dev, axis=-1, keepdims=True)
            y = dev * lax.rsqrt(var + GN_EPS) * gn_ref[:, lanes]
            cat_scr[rows, lanes] = (jax.nn.silu(gate) * y).astype(BF16)

    pext_scr[MAX_WINDOW:MAX_WINDOW + TOKEN_TILE, :] = z_scr[:, 4 * RET_WIDTH:4 * RET_WIDTH + POOL_WIDTH]
    pos = lax.broadcasted_iota(jnp.int32, (TOKEN_TILE, POOL_GROUP_DIM), 0) + seq_tile * TOKEN_TILE
    for gi, w in enumerate(POOL_WINDOWS):
        lanes = slice(gi * POOL_GROUP_DIM, (gi + 1) * POOL_GROUP_DIM)
        tok = pext_scr[MAX_WINDOW:MAX_WINDOW + TOKEN_TILE, lanes]
        total = tok
        for lag in range(1, w):
            total = total + pext_scr[MAX_WINDOW - lag:MAX_WINDOW - lag + TOKEN_TILE, lanes]
        count = jnp.minimum(pos + 1, w).astype(F32)
        pooled = total / count - tok
        mixed = _dot(pooled.astype(BF16), wpool_ref[gi]) * pscale_ref[:, lanes]
        cat_scr[:, RET_WIDTH + gi * POOL_GROUP_DIM:RET_WIDTH + (gi + 1) * POOL_GROUP_DIM] = mixed.astype(BF16)
    pext_scr[0:MAX_WINDOW, :] = pext_scr[TOKEN_TILE:TOKEN_TILE + MAX_WINDOW, :]

    o_ref[0] = x + _dot(cat_scr[...], wout_ref[...])


def _mixer(x, g, w_in, cos, sin, dmask, qdec, kdec, gn, w_pool, pscale, w_out):
    b, s, _ = x.shape
    tile = pl.BlockSpec((1, TOKEN_TILE, D_MODEL), lambda bi, si: (bi, si, 0))
    rot = pl.BlockSpec((TOKEN_TILE, RET_HEAD_DIM), lambda bi, si: (si, 0))
    block_bytes = (4 * TOKEN_TILE * D_MODEL * 4 + D_MODEL * AB_IN_WIDTH * 2 + D_MODEL * D_MODEL * 2
                   + TOKEN_TILE * AB_IN_WIDTH * 4 + 4 * RET_TILE * RET_TILE * 4)
    return pl.pallas_call(
        _mixer_kernel,
        grid=(b, s // TOKEN_TILE),
        in_specs=[
            tile, _resident((1, D_MODEL)), _resident((D_MODEL, AB_IN_WIDTH)), rot, rot,
            _resident((RET_HEADS, RET_TILE, RET_TILE)),
            _resident((RET_HEADS, RET_TILE, RET_HEAD_DIM)),
            _resident((RET_HEADS, RET_TILE, RET_HEAD_DIM)),
            _resident((1, RET_WIDTH)),
            _resident((len(POOL_WINDOWS), POOL_GROUP_DIM, POOL_GROUP_DIM)),
            _resident((1, POOL_WIDTH)), _resident((D_MODEL, D_MODEL)),
        ],
        out_specs=tile,
        out_shape=jax.ShapeDtypeStruct(x.shape, F32),
        scratch_shapes=[
            pltpu.VMEM((TOKEN_TILE, AB_IN_WIDTH), F32),
            pltpu.VMEM((TOKEN_TILE, D_MODEL), BF16),
            pltpu.VMEM((RET_HEADS, RET_HEAD_DIM, RET_HEAD_DIM), F32),
            pltpu.VMEM((TOKEN_TILE + MAX_WINDOW, POOL_WIDTH), F32),
        ],
        compiler_params=_params(2, block_bytes),
        name="mixer_even",
    )(x, g, w_in, cos, sin, dmask, qdec, kdec, gn, w_pool, pscale, w_out)


def _qkv_kernel(x_ref, g_ref, w_ref, q_ref, k_ref, v_ref):
    h = _rms(x_ref[...], g_ref[...]).astype(BF16)
    q_ref[...] = (_dot(h, w_ref[:, 0:D_MODEL]) * (ATT_HEAD_DIM ** -0.5)).astype(BF16)
    k_ref[...] = _dot(h, w_ref[:, D_MODEL:2 * D_MODEL]).astype(BF16)
    v_ref[...] = _dot(h, w_ref[:, 2 * D_MODEL:3 * D_MODEL]).astype(BF16)


def _qkv(x2, g, w):
    n = x2.shape[0]
    tile = pl.BlockSpec((TOKEN_TILE, D_MODEL), lambda i: (i, 0))
    out = jax.ShapeDtypeStruct((n, D_MODEL), BF16)
    block_bytes = 2 * TOKEN_TILE * D_MODEL * 4 + 6 * TOKEN_TILE * D_MODEL * 2 + 3 * D_MODEL * D_MODEL * 2
    return pl.pallas_call(
        _qkv_kernel,
        grid=(n // TOKEN_TILE,),
        in_specs=[tile, _resident((1, D_MODEL)), _resident((D_MODEL, 3 * D_MODEL))],
        out_specs=(tile, tile, tile),
        out_shape=(out, out, out),
        compiler_params=_params(1, block_bytes),
        name="attn_qkv",
    )(x2, g, w)


def _attn_kernel(x_ref, q_ref, kp_ref, kc_ref, vp_ref, vc_ref, bias_ref, wout_ref, o_ref,
                 kcat_scr, vcat_scr, ocat_scr):
    seq_tile = pl.program_id(1)
    kcat_scr[0:TOKEN_TILE, :] = kp_ref[0]
    kcat_scr[TOKEN_TILE:2 * TOKEN_TILE, :] = kc_ref[0]
    vcat_scr[0:TOKEN_TILE, :] = vp_ref[0]
    vcat_scr[TOKEN_TILE:2 * TOKEN_TILE, :] = vc_ref[0]

    lane = lax.broadcasted_iota(jnp.int32, (ATT_GROUP_ROWS, V7X_LANES), 1)
    first_head = lane < ATT_HEAD_DIM
    key_row = lax.broadcasted_iota(jnp.int32, (1, ATT_GROUP_BAND), 1)
    first_valid_row = jnp.where(seq_tile == 0, TOKEN_TILE, 0)

    def group_body(gidx, carry):
        r0 = pl.multiple_of(gidx * ATT_GROUP_ROWS, ATT_GROUP_ROWS)
        pad_bias = jnp.where(key_row + r0 >= first_valid_row, 0.0, NEG_INF)
        for pair in range(ATT_HEADS // 2):
            lanes = slice(pair * V7X_LANES, (pair + 1) * V7X_LANES)
            q = q_ref[0, pl.ds(r0, ATT_GROUP_ROWS), lanes]
            kb = kcat_scr[pl.ds(r0, ATT_GROUP_BAND), lanes]
            vb = vcat_scr[pl.ds(r0, ATT_GROUP_BAND), lanes]
            zero = jnp.zeros_like(q)
            q2 = jnp.concatenate([jnp.where(first_head, q, zero), jnp.where(first_head, zero, q)], axis=0)
            s = _dot_nt(q2, kb)
            bias = jnp.concatenate([bias_ref[2 * pair], bias_ref[2 * pair + 1]], axis=0)
            s = s + bias + pad_bias
            m = jnp.max(s, axis=-1, keepdims=True)
            e = jnp.exp(s - m)
            denom = jnp.sum(e, axis=-1, keepdims=True)
            pv = _dot(e.astype(BF16), vb) * (1.0 / denom)
            out = jnp.where(first_head, pv[0:ATT_GROUP_ROWS], pv[ATT_GROUP_ROWS:2 * ATT_GROUP_ROWS])
            ocat_scr[pl.ds(r0, ATT_GROUP_ROWS), lanes] = out.astype(BF16)
        return carry

    lax.fori_loop(0, TOKEN_TILE // ATT_GROUP_ROWS, group_body, 0)
    o_ref[0] = x_ref[0] + _dot(ocat_scr[...], wout_ref[...])


def _attn(x, q, k, v, bias, w_out):
    b, s, _ = x.shape
    cur = lambda bi, si: (bi, si, 0)
    prev = lambda bi, si: (bi, jnp.maximum(si - 1, 0), 0)
    tile = lambda index_map: pl.BlockSpec((1, TOKEN_TILE, D_MODEL), index_map)
    block_bytes = (4 * TOKEN_TILE * D_MODEL * 4 + 10 * TOKEN_TILE * D_MODEL * 2 + D_MODEL * D_MODEL * 2
                   + ATT_HEADS * ATT_GROUP_ROWS * ATT_GROUP_BAND * 4 + 5 * TOKEN_TILE * D_MODEL * 2)
    return pl.pallas_call(
        _attn_kernel,
        grid=(b, s // TOKEN_TILE),
        in_specs=[
            tile(cur), tile(cur), tile(prev), tile(cur), tile(prev), tile(cur),
            _resident((ATT_HEADS, ATT_GROUP_ROWS, ATT_GROUP_BAND)), _resident((D_MODEL, D_MODEL)),
        ],
        out_specs=tile(cur),
        out_shape=jax.ShapeDtypeStruct(x.shape, F32),
        scratch_shapes=[
            pltpu.VMEM((2 * TOKEN_TILE, D_MODEL), BF16),
            pltpu.VMEM((2 * TOKEN_TILE, D_MODEL), BF16),
            pltpu.VMEM((TOKEN_TILE, D_MODEL), BF16),
        ],
        compiler_params=_params(2, block_bytes),
        name="attn_band",
    )(x, q, k, k, v, v, bias, w_out)


def kernel(x, mix_norm, ffn_norm, w_ffn_in, w_ffn_out, ab_w_in, ab_gn_gain, ab_w_pool, ab_pool_scale,
           ab_w_out, c_w_qkv, c_rel_bias, c_w_out, final_norm):
    b, s, d = x.shape
    depth = mix_norm.shape[0]
    assert d == D_MODEL and s % TOKEN_TILE == 0, (x.shape,)
    n = b * s

    cos, sin = _rotary_tables(s)
    dmask, qdec, kdec = _decay_tables()
    fg = final_norm.reshape(1, D_MODEL)

    for layer in range(depth):
        i = layer // 2
        g_mix = mix_norm[layer].reshape(1, D_MODEL)
        if layer % 2 == 0:
            x = _mixer(x, g_mix, ab_w_in[i].astype(BF16), cos, sin, dmask, qdec, kdec,
                       ab_gn_gain[i].reshape(1, RET_WIDTH), ab_w_pool[i].astype(BF16),
                       ab_pool_scale[i].reshape(1, POOL_WIDTH), ab_w_out[i].astype(BF16))
        else:
            q, k, v = _qkv(x.reshape(n, d), g_mix, c_w_qkv[i].astype(BF16))
            shape3 = (b, s, d)
            x = _attn(x, q.reshape(shape3), k.reshape(shape3), v.reshape(shape3),
                      _bias_table(c_rel_bias[i]), c_w_out[i].astype(BF16))
        x = _ffn(x.reshape(n, d), ffn_norm[layer].reshape(1, D_MODEL), w_ffn_in[layer].astype(BF16),
                 w_ffn_out[layer].astype(BF16), fg, final_norm=(layer == depth - 1)).reshape(b, s, d)
    return x
```

```python
import functools
import math

import jax
import jax.numpy as jnp
from jax import lax
from jax.experimental import pallas as pl
from jax.experimental.pallas import tpu as pltpu

F32 = jnp.float32
BF16 = jnp.bfloat16

D_MODEL = 1024
D_FF = 4 * D_MODEL
CHUNK = 64
RMS_EPS = 1e-6
RET_WIDTH = 512
RET_HEADS = 4
RET_HEAD_DIM = 128
RET_ROPE_BASE = 10000.0
GN_EPS = 1e-5
POOL_WIDTH = 512
POOL_WINDOWS = (2, 4, 8, 16)
POOL_GROUP_DIM = 128
MAX_WINDOW = max(POOL_WINDOWS)
AB_IN_WIDTH = 4 * RET_WIDTH + POOL_WIDTH
ATT_HEADS = 16
ATT_HEAD_DIM = 64
LEFT_CHUNKS = 8
BAND = (LEFT_CHUNKS + 1) * CHUNK
REL_CLIP = 128
N_REL = 2 * REL_CLIP + 1
NEG_INF = -1e30

V7X_LANES = 128
V7X_VMEM_BYTES = 64 * 1024 * 1024
V7X_VMEM_HEADROOM_BYTES = 8 * 1024 * 1024

TOKEN_TILE = 512
RET_TILE = 256
FFN_HIDDEN_TILE = 1024
ATT_GROUP_CHUNKS = 2
ATT_GROUP_ROWS = ATT_GROUP_CHUNKS * CHUNK
ATT_GROUP_BAND = BAND + (ATT_GROUP_CHUNKS - 1) * CHUNK
ATT_PAIRS = ATT_HEADS // 2
ATT_ITEMS = (TOKEN_TILE // ATT_GROUP_ROWS) * ATT_PAIRS
ATT_UNROLL = 8
ATT_LOOKAHEAD = 2
ATT_SCORE_BUFS = 4
ATT_PROB_BUFS = 2
ATT_SOFTMAX_ROWS = 32
LOG2_E = math.log2(math.e)
REL_PAD = 3 * V7X_LANES
ROT_TABLE_TILE = 1024


def _vmem_limit(block_bytes):
    return int(min(block_bytes + 24 * 1024 * 1024, V7X_VMEM_BYTES - V7X_VMEM_HEADROOM_BYTES))


def _params(n_axes, block_bytes, flags=None):
    return pltpu.CompilerParams(
        dimension_semantics=("arbitrary",) * n_axes,
        vmem_limit_bytes=_vmem_limit(block_bytes),
        flags=flags,
    )


def _resident(shape):
    nd = len(shape)
    return pl.BlockSpec(shape, lambda *_: (0,) * nd, pipeline_mode=pl.Buffered(1))


def _rms(x, g):
    return x * lax.rsqrt(jnp.mean(x * x, axis=-1, keepdims=True) + RMS_EPS) * g


def _dot(a, b):
    return jnp.dot(a, b, preferred_element_type=F32)


def _dot_nt(a, b):
    return lax.dot_general(a, b, (((1,), (1,)), ((), ())), preferred_element_type=F32)


def _dot_tn(a, b):
    return lax.dot_general(a, b, (((0,), (0,)), ((), ())), preferred_element_type=F32)


def _rotary_table_kernel(cos_ref, sin_ref):
    rows = cos_ref.shape[0]
    n = (lax.broadcasted_iota(jnp.int32, (rows, RET_HEAD_DIM), 0) + pl.program_id(0) * rows).astype(F32)
    lane = lax.broadcasted_iota(jnp.int32, (rows, RET_HEAD_DIM), 1)
    frac = (lane >> 1).astype(F32) / float(RET_HEAD_DIM // 2 - 1)
    inv_freq = 1.0 / jnp.exp(frac * math.log(RET_ROPE_BASE))
    ang = n * inv_freq
    cos_ref[...] = jnp.cos(ang)
    sin_ref[...] = jnp.where((lane & 1) == 0, -1.0, 1.0) * jnp.sin(ang)


def _rotary_tables(seq):
    tile = min(ROT_TABLE_TILE, seq)
    shape = jax.ShapeDtypeStruct((seq, RET_HEAD_DIM), F32)
    spec = pl.BlockSpec((tile, RET_HEAD_DIM), lambda i: (i, 0))
    return pl.pallas_call(
        _rotary_table_kernel,
        grid=(seq // tile,),
        out_specs=(spec, spec),
        out_shape=(shape, shape),
        compiler_params=_params(1, 0),
        name="rotary_tables",
    )()


def _ret_log_gamma(h):
    return math.log1p(-(2.0 ** (-5.0 - h)))


def _decay_table_kernel(dmask_ref, qdec_ref, kdec_ref):
    t = lax.broadcasted_iota(jnp.int32, (RET_TILE, RET_TILE), 0)
    m = lax.broadcasted_iota(jnp.int32, (RET_TILE, RET_TILE), 1)
    dist = jnp.abs(t - m).astype(F32)
    chunk_shift = CHUNK.bit_length() - 1
    visible = (m >> chunk_shift) <= (t >> chunk_shift)
    tq = lax.broadcasted_iota(jnp.int32, (RET_TILE, RET_HEAD_DIM), 0).astype(F32)
    for h in range(RET_HEADS):
        lg = _ret_log_gamma(h)
        dmask_ref[h] = jnp.where(visible, jnp.exp(dist * lg), 0.0)
        qdec_ref[h] = jnp.exp((tq + 1.0) * lg)
        kdec_ref[h] = jnp.exp((RET_TILE - 1.0 - tq) * lg)


def _decay_tables():
    return pl.pallas_call(
        _decay_table_kernel,
        out_shape=(
            jax.ShapeDtypeStruct((RET_HEADS, RET_TILE, RET_TILE), F32),
            jax.ShapeDtypeStruct((RET_HEADS, RET_TILE, RET_HEAD_DIM), F32),
            jax.ShapeDtypeStruct((RET_HEADS, RET_TILE, RET_HEAD_DIM), F32),
        ),
        name="decay_tables",
    )()


_BIAS_ROW_TILE = 8
_BIAS_LEFT_PAD = CHUNK
_BIAS_EXT = 7 * V7X_LANES
_BIAS_WIDE = 6 * V7X_LANES


def _split3(x):
    hi = x.astype(BF16)
    r1 = x - hi.astype(F32)
    mid = r1.astype(BF16)
    lo = (r1 - mid.astype(F32)).astype(BF16)
    return hi, mid, lo


def _bias_table_kernel(rel_ref, out_ref):
    r = lax.broadcasted_iota(jnp.int32, (REL_PAD, _BIAS_EXT), 0)
    u = lax.broadcasted_iota(jnp.int32, (REL_PAD, _BIAS_EXT), 1)
    d = u - _BIAS_LEFT_PAD
    rel_idx = jnp.minimum(LEFT_CHUNKS * CHUNK + (CHUNK - 1) - d, REL_CLIP) + REL_CLIP
    inside = (d >= 0) & (d < BAND + CHUNK - 1)
    sel = jnp.where(inside & (r == rel_idx), 1.0, 0.0).astype(BF16)
    hi, mid, lo = _split3(rel_ref[...])
    diag = _dot(hi, sel) + _dot(mid, sel) + _dot(lo, sel)

    sub = lax.broadcasted_iota(jnp.int32, (_BIAS_ROW_TILE, _BIAS_WIDE), 0)
    col = lax.broadcasted_iota(jnp.int32, (_BIAS_ROW_TILE, ATT_GROUP_BAND), 1)
    for h in range(ATT_HEADS):
        row = diag[h:h + 1, :]
        skew = jnp.zeros((_BIAS_ROW_TILE, _BIAS_WIDE), F32)
        for b in range(_BIAS_ROW_TILE):
            start = _BIAS_ROW_TILE - 1 - b
            piece = jnp.broadcast_to(row[:, start:start + _BIAS_WIDE], (_BIAS_ROW_TILE, _BIAS_WIDE))
            skew = jnp.where(sub == b, piece, skew)
        for c in range(ATT_GROUP_CHUNKS):
            visible = (col >= c * CHUNK) & (col < c * CHUNK + BAND)
            for a in range(CHUNK // _BIAS_ROW_TILE):
                start = _BIAS_LEFT_PAD + (CHUNK - _BIAS_ROW_TILE) - c * CHUNK - _BIAS_ROW_TILE * a
                tile = skew[:, start:start + ATT_GROUP_BAND]
                r0 = c * CHUNK + _BIAS_ROW_TILE * a
                out_ref[h, r0:r0 + _BIAS_ROW_TILE, :] = jnp.where(visible, tile * LOG2_E, NEG_INF)


def _bias_table(rel_bias):
    rel = jnp.pad(rel_bias, ((0, 0), (0, REL_PAD - N_REL)))
    return pl.pallas_call(
        _bias_table_kernel,
        out_shape=jax.ShapeDtypeStruct((ATT_HEADS, ATT_GROUP_ROWS, ATT_GROUP_BAND), F32),
        name="bias_table",
    )(rel)


def _ffn_kernel(x_ref, g_ref, w1_ref, w2_ref, fg_ref, o_ref, *, final_norm):
    x = x_ref[...]
    h = _rms(x, g_ref[...]).astype(BF16)
    acc = x
    for j in range(D_FF // FFN_HIDDEN_TILE):
        cols = slice(j * FFN_HIDDEN_TILE, (j + 1) * FFN_HIDDEN_TILE)
        a = jnp.maximum(_dot(h, w1_ref[:, cols]), 0.0)
        acc = acc + _dot((a * a).astype(BF16), w2_ref[cols, :])
    if final_norm:
        acc = _rms(acc, fg_ref[...])
    o_ref[...] = acc


def _ffn(x2, g, w1, w2, fg, final_norm):
    n = x2.shape[0]
    tile = pl.BlockSpec((TOKEN_TILE, D_MODEL), lambda i: (i, 0))
    vec = _resident((1, D_MODEL))
    block_bytes = 4 * TOKEN_TILE * D_MODEL * 4 + 2 * D_MODEL * D_FF * 2
    return pl.pallas_call(
        functools.partial(_ffn_kernel, final_norm=final_norm),
        grid=(n // TOKEN_TILE,),
        in_specs=[tile, vec, _resident((D_MODEL, D_FF)), _resident((D_FF, D_MODEL)), vec],
        out_specs=tile,
        out_shape=jax.ShapeDtypeStruct((n, D_MODEL), F32),
        compiler_params=_params(1, block_bytes),
        name="ffn",
    )(x2, g, w1, w2, fg)


def _swap_pairs(x):
    lane = lax.broadcasted_iota(jnp.int32, x.shape, 1)
    return jnp.where((lane & 1) == 0, pltpu.roll(x, V7X_LANES - 1, 1), pltpu.roll(x, 1, 1))


def _mixer_kernel(x_ref, g_ref, win_ref, cos_ref, sin_ref, dmask_ref, qdec_ref, kdec_ref,
                  gn_ref, wpool_ref, pscale_ref, wout_ref, o_ref,
                  z_scr, cat_scr, state_scr, pext_scr):
    seq_tile = pl.program_id(1)

    @pl.when(seq_tile == 0)
    def _():
        state_scr[...] = jnp.zeros_like(state_scr)
        pext_scr[0:MAX_WINDOW, :] = jnp.zeros((MAX_WINDOW, POOL_WIDTH), F32)

    x = x_ref[0]
    h = _rms(x, g_ref[...]).astype(BF16)
    z_scr[...] = _dot(h, win_ref[...])

    for hd in range(RET_HEADS):
        lanes = slice(hd * RET_HEAD_DIM, (hd + 1) * RET_HEAD_DIM)
        state_decay = math.exp(RET_TILE * _ret_log_gamma(hd))
        for st in range(TOKEN_TILE // RET_TILE):
            rows = slice(st * RET_TILE, (st + 1) * RET_TILE)
            cos = cos_ref[rows, :]
            sin = sin_ref[rows, :]
            q = z_scr[rows, lanes]
            k = z_scr[rows, RET_WIDTH + hd * RET_HEAD_DIM:RET_WIDTH + (hd + 1) * RET_HEAD_DIM]
            v = z_scr[rows, 2 * RET_WIDTH + hd * RET_HEAD_DIM:2 * RET_WIDTH + (hd + 1) * RET_HEAD_DIM]
            gate = z_scr[rows, 3 * RET_WIDTH + hd * RET_HEAD_DIM:3 * RET_WIDTH + (hd + 1) * RET_HEAD_DIM]
            q = q * cos + _swap_pairs(q) * sin
            k = (k * cos + _swap_pairs(k) * sin) * (RET_HEAD_DIM ** -0.5)
            vb = v.astype(BF16)
            scores = _dot_nt(q.astype(BF16), k.astype(BF16)) * dmask_ref[hd]
            state = state_scr[hd]
            o = _dot(scores.astype(BF16), vb) + _dot((q * qdec_ref[hd]).astype(BF16), state.astype(BF16))
            state_scr[hd] = state * state_decay + _dot_tn((k * kdec_ref[hd]).astype(BF16), vb)
            mu = jnp.mean(o, axis=-1, keepdims=True)
            dev = o - mu
            var = jnp.mean(dev * dev, axis=-1, keepdims=True)
            y = dev * lax.rsqrt(var + GN_EPS) * gn_ref[:, lanes]
            cat_scr[rows, lanes] = (jax.nn.silu(gate) * y).astype(BF16)

    pext_scr[MAX_WINDOW:MAX_WINDOW + TOKEN_TILE, :] = z_scr[:, 4 * RET_WIDTH:4 * RET_WIDTH + POOL_WIDTH]
    pos = lax.broadcasted_iota(jnp.int32, (TOKEN_TILE, POOL_GROUP_DIM), 0) + seq_tile * TOKEN_TILE
    for gi, w in enumerate(POOL_WINDOWS):
        lanes = slice(gi * POOL_GROUP_DIM, (gi + 1) * POOL_GROUP_DIM)
        tok = pext_scr[MAX_WINDOW:MAX_WINDOW + TOKEN_TILE, lanes]
        total = tok
        for lag in range(1, w):
            total = total + pext_scr[MAX_WINDOW - lag:MAX_WINDOW - lag + TOKEN_TILE, lanes]
        count = jnp.minimum(pos + 1, w).astype(F32)
        pooled = total / count - tok
        mixed = _dot(pooled.astype(BF16), wpool_ref[gi]) * pscale_ref[:, lanes]
        cat_scr[:, RET_WIDTH + gi * POOL_GROUP_DIM:RET_WIDTH + (gi + 1) * POOL_GROUP_DIM] = mixed.astype(BF16)
    pext_scr[0:MAX_WINDOW, :] = pext_scr[TOKEN_TILE:TOKEN_TILE + MAX_WINDOW, :]

    o_ref[0] = x + _dot(cat_scr[...], wout_ref[...])


def _mixer(x, g, w_in, cos, sin, dmask, qdec, kdec, gn, w_pool, pscale, w_out):
    b, s, _ = x.shape
    tile = pl.BlockSpec((1, TOKEN_TILE, D_MODEL), lambda bi, si: (bi, si, 0))
    rot = pl.BlockSpec((TOKEN_TILE, RET_HEAD_DIM), lambda bi, si: (si, 0))
    block_bytes = (4 * TOKEN_TILE * D_MODEL * 4 + D_MODEL * AB_IN_WIDTH * 2 + D_MODEL * D_MODEL * 2
                   + TOKEN_TILE * AB_IN_WIDTH * 4 + 4 * RET_TILE * RET_TILE * 4)
    return pl.pallas_call(
        _mixer_kernel,
        grid=(b, s // TOKEN_TILE),
        in_specs=[
            tile, _resident((1, D_MODEL)), _resident((D_MODEL, AB_IN_WIDTH)), rot, rot,
            _resident((RET_HEADS, RET_TILE, RET_TILE)),
            _resident((RET_HEADS, RET_TILE, RET_HEAD_DIM)),
            _resident((RET_HEADS, RET_TILE, RET_HEAD_DIM)),
            _resident((1, RET_WIDTH)),
            _resident((len(POOL_WINDOWS), POOL_GROUP_DIM, POOL_GROUP_DIM)),
            _resident((1, POOL_WIDTH)), _resident((D_MODEL, D_MODEL)),
        ],
        out_specs=tile,
        out_shape=jax.ShapeDtypeStruct(x.shape, F32),
        scratch_shapes=[
            pltpu.VMEM((TOKEN_TILE, AB_IN_WIDTH), F32),
            pltpu.VMEM((TOKEN_TILE, D_MODEL), BF16),
            pltpu.VMEM((RET_HEADS, RET_HEAD_DIM, RET_HEAD_DIM), F32),
            pltpu.VMEM((TOKEN_TILE + MAX_WINDOW, POOL_WIDTH), F32),
        ],
        compiler_params=_params(2, block_bytes),
        name="mixer_even",
    )(x, g, w_in, cos, sin, dmask, qdec, kdec, gn, w_pool, pscale, w_out)


def _qkv_kernel(x_ref, g_ref, w_ref, q_ref, k_ref, v_ref):
    h = _rms(x_ref[0], g_ref[...]).astype(BF16)
    for ref, col0, scale in ((q_ref, 0, ATT_HEAD_DIM ** -0.5 * LOG2_E), (k_ref, D_MODEL, None),
                             (v_ref, 2 * D_MODEL, None)):
        z = _dot(h, w_ref[:, col0:col0 + D_MODEL])
        if scale is not None:
            z = z * scale
        for pair in range(ATT_PAIRS):
            ref[0, pair] = z[:, pair * V7X_LANES:(pair + 1) * V7X_LANES].astype(BF16)


def _qkv(x, g, w):
    b, s, _ = x.shape
    tile = pl.BlockSpec((1, TOKEN_TILE, D_MODEL), lambda bi, si: (bi, si, 0))
    pair_tile = pl.BlockSpec((1, ATT_PAIRS, TOKEN_TILE, V7X_LANES), lambda bi, si: (bi, 0, si, 0))
    out = jax.ShapeDtypeStruct((b, ATT_PAIRS, s, V7X_LANES), BF16)
    block_bytes = 2 * TOKEN_TILE * D_MODEL * 4 + 6 * TOKEN_TILE * D_MODEL * 2 + 3 * D_MODEL * D_MODEL * 2
    return pl.pallas_call(
        _qkv_kernel,
        grid=(b, s // TOKEN_TILE),
        in_specs=[tile, _resident((1, D_MODEL)), _resident((D_MODEL, 3 * D_MODEL))],
        out_specs=(pair_tile, pair_tile, pair_tile),
        out_shape=(out, out, out),
        compiler_params=_params(2, block_bytes),
        name="attn_qkv",
    )(x, g, w)


def _attn_kernel(x_ref, q_ref, kp_ref, kc_ref, vp_ref, vc_ref, bias_ref, wout_ref, o_ref,
                 kcat_scr, vcat_scr, opair_scr, ocat_scr, *bufs):
    seq_tile = pl.program_id(1)
    kcat_scr[:, 0:TOKEN_TILE, :] = kp_ref[0]
    kcat_scr[:, TOKEN_TILE:2 * TOKEN_TILE, :] = kc_ref[0]
    vcat_scr[:, 0:TOKEN_TILE, :] = vp_ref[0]
    vcat_scr[:, TOKEN_TILE:2 * TOKEN_TILE, :] = vc_ref[0]

    lane = lax.broadcasted_iota(jnp.int32, (ATT_GROUP_ROWS, V7X_LANES), 1)
    first_head = lane < ATT_HEAD_DIM
    key_row = lax.broadcasted_iota(jnp.int32, (1, ATT_GROUP_BAND), 1)

    def item(t):
        r0 = (t >> (ATT_PAIRS.bit_length() - 1)) * ATT_GROUP_ROWS
        if not isinstance(t, int):
            r0 = pl.multiple_of(r0, ATT_GROUP_ROWS)
        return r0, t & (ATT_PAIRS - 1)

    def scores(t, s_ref, first_tile):
        r0, pair = item(t)
        q = q_ref[0, pair, pl.ds(r0, ATT_GROUP_ROWS), :]
        zero = jnp.zeros_like(q)
        q2 = jnp.concatenate([jnp.where(first_head, q, zero), jnp.where(first_head, zero, q)], axis=0)
        s = _dot_nt(q2, kcat_scr[pair, pl.ds(r0, ATT_GROUP_BAND), :]) + bias_ref[pair]
        if first_tile:
            s = s + jnp.where(key_row + r0 >= TOKEN_TILE, 0.0, NEG_INF)
        s_ref[...] = s

    def attend(t, s_ref, p_ref):
        r0, pair = item(t)
        inv_denoms = []
        for rb in range(0, 2 * ATT_GROUP_ROWS, ATT_SOFTMAX_ROWS):
            s = s_ref[rb:rb + ATT_SOFTMAX_ROWS, :]
            e = jnp.exp2(s - jnp.max(s, axis=-1, keepdims=True))
            inv_denoms.append(1.0 / jnp.sum(e, axis=-1, keepdims=True))
            p_ref[rb:rb + ATT_SOFTMAX_ROWS, :] = e.astype(BF16)
        pv = _dot(p_ref[...], vcat_scr[pair, pl.ds(r0, ATT_GROUP_BAND), :])
        pv = pv * jnp.concatenate(inv_denoms, axis=0)
        out = jnp.where(first_head, pv[0:ATT_GROUP_ROWS], pv[ATT_GROUP_ROWS:2 * ATT_GROUP_ROWS])
        opair_scr[pair, pl.ds(r0, ATT_GROUP_ROWS), :] = out.astype(BF16)

    def run(first_tile):
        score_scrs = bufs[:ATT_SCORE_BUFS]
        prob_scrs = bufs[ATT_SCORE_BUFS:]
        n_bufs = len(score_scrs)

        def block(t0, last):
            for u in range(ATT_UNROLL):
                if not (last and u + ATT_LOOKAHEAD >= ATT_UNROLL):
                    scores(t0 + u + ATT_LOOKAHEAD, score_scrs[(u + ATT_LOOKAHEAD) % n_bufs], first_tile)
                attend(t0 + u, score_scrs[u % n_bufs], prob_scrs[u % len(prob_scrs)])

        def body(j, carry):
            block(j * ATT_UNROLL, False)
            return carry

        for t in range(ATT_LOOKAHEAD):
            scores(t, score_scrs[t], first_tile)
        lax.fori_loop(0, ATT_ITEMS // ATT_UNROLL - 1, body, 0)
        block(ATT_ITEMS - ATT_UNROLL, True)

    pl.when(seq_tile == 0)(functools.partial(run, True))
    pl.when(seq_tile != 0)(functools.partial(run, False))

    for pair in range(ATT_PAIRS):
        ocat_scr[:, pair * V7X_LANES:(pair + 1) * V7X_LANES] = opair_scr[pair]
    o_ref[0] = x_ref[0] + _dot(ocat_scr[...], wout_ref[...])


def _attn(x, q, k, v, bias, w_out):
    b, s, _ = x.shape
    tile = pl.BlockSpec((1, TOKEN_TILE, D_MODEL), lambda bi, si: (bi, si, 0))
    cur = pl.BlockSpec((1, ATT_PAIRS, TOKEN_TILE, V7X_LANES), lambda bi, si: (bi, 0, si, 0))
    prev = pl.BlockSpec((1, ATT_PAIRS, TOKEN_TILE, V7X_LANES),
                        lambda bi, si: (bi, 0, jnp.maximum(si - 1, 0), 0))
    score_buf = pltpu.VMEM((2 * ATT_GROUP_ROWS, ATT_GROUP_BAND), F32)
    prob_buf = pltpu.VMEM((2 * ATT_GROUP_ROWS, ATT_GROUP_BAND), BF16)
    block_bytes = (4 * TOKEN_TILE * D_MODEL * 4 + 10 * TOKEN_TILE * D_MODEL * 2 + D_MODEL * D_MODEL * 2
                   + ATT_HEADS * ATT_GROUP_ROWS * ATT_GROUP_BAND * 4 + 6 * TOKEN_TILE * D_MODEL * 2
                   + (4 * ATT_SCORE_BUFS + 2 * ATT_PROB_BUFS) * 2 * ATT_GROUP_ROWS * ATT_GROUP_BAND)
    return pl.pallas_call(
        _attn_kernel,
        grid=(b, s // TOKEN_TILE),
        in_specs=[
            tile, cur, prev, cur, prev, cur,
            _resident((ATT_PAIRS, 2 * ATT_GROUP_ROWS, ATT_GROUP_BAND)), _resident((D_MODEL, D_MODEL)),
        ],
        out_specs=tile,
        out_shape=jax.ShapeDtypeStruct(x.shape, F32),
        scratch_shapes=[
            pltpu.VMEM((ATT_PAIRS, 2 * TOKEN_TILE, V7X_LANES), BF16),
            pltpu.VMEM((ATT_PAIRS, 2 * TOKEN_TILE, V7X_LANES), BF16),
            pltpu.VMEM((ATT_PAIRS, TOKEN_TILE, V7X_LANES), BF16),
            pltpu.VMEM((TOKEN_TILE, D_MODEL), BF16),
        ] + [score_buf] * ATT_SCORE_BUFS + [prob_buf] * ATT_PROB_BUFS,
        compiler_params=_params(2, block_bytes),
        name="attn_band",
    )(x, q, k, k, v, v, bias, w_out)


def kernel(x, mix_norm, ffn_norm, w_ffn_in, w_ffn_out, ab_w_in, ab_gn_gain, ab_w_pool, ab_pool_scale,
           ab_w_out, c_w_qkv, c_rel_bias, c_w_out, final_norm):
    b, s, d = x.shape
    depth = mix_norm.shape[0]
    assert d == D_MODEL and s % TOKEN_TILE == 0, (x.shape,)
    n = b * s

    cos, sin = _rotary_tables(s)
    dmask, qdec, kdec = _decay_tables()
    fg = final_norm.reshape(1, D_MODEL)

    for layer in range(depth):
        i = layer // 2
        g_mix = mix_norm[layer].reshape(1, D_MODEL)
        if layer % 2 == 0:
            x = _mixer(x, g_mix, ab_w_in[i].astype(BF16), cos, sin, dmask, qdec, kdec,
                       ab_gn_gain[i].reshape(1, RET_WIDTH), ab_w_pool[i].astype(BF16),
                       ab_pool_scale[i].reshape(1, POOL_WIDTH), ab_w_out[i].astype(BF16))
        else:
            q, k, v = _qkv(x, g_mix, c_w_qkv[i].astype(BF16))
            bias = _bias_table(c_rel_bias[i]).reshape(ATT_PAIRS, 2 * ATT_GROUP_ROWS, ATT_GROUP_BAND)
            x = _attn(x, q, k, v, bias, c_w_out[i].astype(BF16))
        x = _ffn(x.reshape(n, d), ffn_norm[layer].reshape(1, D_MODEL), w_ffn_in[layer].astype(BF16),
                 w_ffn_out[layer].astype(BF16), fg, final_norm=(layer == depth - 1)).reshape(b, s, d)
    return x
```

```python
import functools
import math

import jax
import jax.numpy as jnp
from jax import lax
from jax.experimental import pallas as pl
from jax.experimental.pallas import tpu as pltpu

F32 = jnp.float32
BF16 = jnp.bfloat16

D_MODEL = 1024
D_FF = 4 * D_MODEL
CHUNK = 64
RMS_EPS = 1e-6
RET_WIDTH = 512
RET_HEADS = 4
RET_HEAD_DIM = 128
RET_ROPE_BASE = 10000.0
GN_EPS = 1e-5
POOL_WIDTH = 512
POOL_WINDOWS = (2, 4, 8, 16)
POOL_GROUP_DIM = 128
MAX_WINDOW = max(POOL_WINDOWS)
AB_IN_WIDTH = 4 * RET_WIDTH + POOL_WIDTH
ATT_HEADS = 16
ATT_HEAD_DIM = 64
LEFT_CHUNKS = 8
BAND = (LEFT_CHUNKS + 1) * CHUNK
REL_CLIP = 128
N_REL = 2 * REL_CLIP + 1
NEG_INF = -1e30

V7X_LANES = 128
V7X_VMEM_BYTES = 64 * 1024 * 1024
V7X_VMEM_HEADROOM_BYTES = 8 * 1024 * 1024

TOKEN_TILE = 512
FFN_TOKEN_TILE = 1024
QKV_TOKEN_TILE = 1024
RET_TILE = 256
FFN_HIDDEN_TILE = 1024
ATT_GROUP_CHUNKS = 2
ATT_GROUP_ROWS = ATT_GROUP_CHUNKS * CHUNK
ATT_GROUP_BAND = BAND + (ATT_GROUP_CHUNKS - 1) * CHUNK
ATT_PAIRS = ATT_HEADS // 2
ATT_GROUPS = TOKEN_TILE // ATT_GROUP_ROWS
ATT_LOOKAHEAD = 2
ATT_SCORE_BUFS = 4
ATT_PROB_BUFS = 2
ATT_SOFTMAX_ROWS = 32
LOG2_E = math.log2(math.e)
REL_PAD = 3 * V7X_LANES
ROT_TABLE_TILE = 1024


def _vmem_limit(block_bytes):
    return int(min(block_bytes + 24 * 1024 * 1024, V7X_VMEM_BYTES - V7X_VMEM_HEADROOM_BYTES))


def _params(n_axes, block_bytes, flags=None):
    return pltpu.CompilerParams(
        dimension_semantics=("arbitrary",) * n_axes,
        vmem_limit_bytes=_vmem_limit(block_bytes),
        flags=flags,
    )


def _resident(shape):
    nd = len(shape)
    return pl.BlockSpec(shape, lambda *_: (0,) * nd, pipeline_mode=pl.Buffered(1))


def _rms(x, g):
    return x * lax.rsqrt(jnp.mean(x * x, axis=-1, keepdims=True) + RMS_EPS) * g


def _dot(a, b):
    return jnp.dot(a, b, preferred_element_type=F32)


def _dot_nt(a, b):
    return lax.dot_general(a, b, (((1,), (1,)), ((), ())), preferred_element_type=F32)


def _dot_tn(a, b):
    return lax.dot_general(a, b, (((0,), (0,)), ((), ())), preferred_element_type=F32)


def _rotary_table_kernel(cos_ref, sin_ref):
    rows = cos_ref.shape[0]
    n = (lax.broadcasted_iota(jnp.int32, (rows, RET_HEAD_DIM), 0) + pl.program_id(0) * rows).astype(F32)
    lane = lax.broadcasted_iota(jnp.int32, (rows, RET_HEAD_DIM), 1)
    frac = (lane >> 1).astype(F32) / float(RET_HEAD_DIM // 2 - 1)
    inv_freq = 1.0 / jnp.exp(frac * math.log(RET_ROPE_BASE))
    ang = n * inv_freq
    cos_ref[...] = jnp.cos(ang)
    sin_ref[...] = jnp.where((lane & 1) == 0, -1.0, 1.0) * jnp.sin(ang)


def _rotary_tables(seq):
    tile = min(ROT_TABLE_TILE, seq)
    shape = jax.ShapeDtypeStruct((seq, RET_HEAD_DIM), F32)
    spec = pl.BlockSpec((tile, RET_HEAD_DIM), lambda i: (i, 0))
    return pl.pallas_call(
        _rotary_table_kernel,
        grid=(seq // tile,),
        out_specs=(spec, spec),
        out_shape=(shape, shape),
        compiler_params=_params(1, 0),
        name="rotary_tables",
    )()


def _ret_log_gamma(h):
    return math.log1p(-(2.0 ** (-5.0 - h)))


def _decay_table_kernel(dmask_ref, qdec_ref, kdec_ref):
    t = lax.broadcasted_iota(jnp.int32, (RET_TILE, RET_TILE), 0)
    m = lax.broadcasted_iota(jnp.int32, (RET_TILE, RET_TILE), 1)
    dist = jnp.abs(t - m).astype(F32)
    chunk_shift = CHUNK.bit_length() - 1
    visible = (m >> chunk_shift) <= (t >> chunk_shift)
    tq = lax.broadcasted_iota(jnp.int32, (RET_TILE, RET_HEAD_DIM), 0).astype(F32)
    for h in range(RET_HEADS):
        lg = _ret_log_gamma(h)
        dmask_ref[h] = jnp.where(visible, jnp.exp(dist * lg), 0.0)
        qdec_ref[h] = jnp.exp((tq + 1.0) * lg)
        kdec_ref[h] = jnp.exp((RET_TILE - 1.0 - tq) * lg)


def _decay_tables():
    return pl.pallas_call(
        _decay_table_kernel,
        out_shape=(
            jax.ShapeDtypeStruct((RET_HEADS, RET_TILE, RET_TILE), F32),
            jax.ShapeDtypeStruct((RET_HEADS, RET_TILE, RET_HEAD_DIM), F32),
            jax.ShapeDtypeStruct((RET_HEADS, RET_TILE, RET_HEAD_DIM), F32),
        ),
        name="decay_tables",
    )()


_BIAS_ROW_TILE = 8
_BIAS_LEFT_PAD = CHUNK
_BIAS_EXT = 7 * V7X_LANES
_BIAS_WIDE = 6 * V7X_LANES


def _split3(x):
    hi = x.astype(BF16)
    r1 = x - hi.astype(F32)
    mid = r1.astype(BF16)
    lo = (r1 - mid.astype(F32)).astype(BF16)
    return hi, mid, lo


def _bias_table_kernel(rel_ref, out_ref):
    r = lax.broadcasted_iota(jnp.int32, (REL_PAD, _BIAS_EXT), 0)
    u = lax.broadcasted_iota(jnp.int32, (REL_PAD, _BIAS_EXT), 1)
    d = u - _BIAS_LEFT_PAD
    rel_idx = jnp.minimum(LEFT_CHUNKS * CHUNK + (CHUNK - 1) - d, REL_CLIP) + REL_CLIP
    inside = (d >= 0) & (d < BAND + CHUNK - 1)
    sel = jnp.where(inside & (r == rel_idx), 1.0, 0.0).astype(BF16)
    hi, mid, lo = _split3(rel_ref[...])
    diag = _dot(hi, sel) + _dot(mid, sel) + _dot(lo, sel)

    sub = lax.broadcasted_iota(jnp.int32, (_BIAS_ROW_TILE, _BIAS_WIDE), 0)
    col = lax.broadcasted_iota(jnp.int32, (_BIAS_ROW_TILE, ATT_GROUP_BAND), 1)
    for h in range(ATT_HEADS):
        row = diag[h:h + 1, :]
        skew = jnp.zeros((_BIAS_ROW_TILE, _BIAS_WIDE), F32)
        for b in range(_BIAS_ROW_TILE):
            start = _BIAS_ROW_TILE - 1 - b
            piece = jnp.broadcast_to(row[:, start:start + _BIAS_WIDE], (_BIAS_ROW_TILE, _BIAS_WIDE))
            skew = jnp.where(sub == b, piece, skew)
        for c in range(ATT_GROUP_CHUNKS):
            visible = (col >= c * CHUNK) & (col < c * CHUNK + BAND)
            for a in range(CHUNK // _BIAS_ROW_TILE):
                start = _BIAS_LEFT_PAD + (CHUNK - _BIAS_ROW_TILE) - c * CHUNK - _BIAS_ROW_TILE * a
                tile = skew[:, start:start + ATT_GROUP_BAND]
                r0 = c * CHUNK + _BIAS_ROW_TILE * a
                out_ref[h, r0:r0 + _BIAS_ROW_TILE, :] = jnp.where(visible, tile * LOG2_E, NEG_INF)


def _bias_table(rel_bias):
    rel = jnp.pad(rel_bias, ((0, 0), (0, REL_PAD - N_REL)))
    return pl.pallas_call(
        _bias_table_kernel,
        out_shape=jax.ShapeDtypeStruct((ATT_HEADS, ATT_GROUP_ROWS, ATT_GROUP_BAND), F32),
        name="bias_table",
    )(rel)


def _ffn_kernel(x_ref, g_ref, w1_ref, w2_ref, fg_ref, o_ref, *, final_norm):
    x = x_ref[...]
    h = _rms(x, g_ref[...]).astype(BF16)
    acc = x
    for j in range(D_FF // FFN_HIDDEN_TILE):
        cols = slice(j * FFN_HIDDEN_TILE, (j + 1) * FFN_HIDDEN_TILE)
        a = jnp.maximum(_dot(h, w1_ref[:, cols]), 0.0)
        acc = acc + _dot((a * a).astype(BF16), w2_ref[cols, :])
    if final_norm:
        acc = _rms(acc, fg_ref[...])
    o_ref[...] = acc


def _ffn(x2, g, w1, w2, fg, final_norm):
    n = x2.shape[0]
    tile = pl.BlockSpec((FFN_TOKEN_TILE, D_MODEL), lambda i: (i, 0))
    vec = _resident((1, D_MODEL))
    block_bytes = 4 * FFN_TOKEN_TILE * D_MODEL * 4 + 2 * D_MODEL * D_FF * 2
    return pl.pallas_call(
        functools.partial(_ffn_kernel, final_norm=final_norm),
        grid=(n // FFN_TOKEN_TILE,),
        in_specs=[tile, vec, _resident((D_MODEL, D_FF)), _resident((D_FF, D_MODEL)), vec],
        out_specs=tile,
        out_shape=jax.ShapeDtypeStruct((n, D_MODEL), F32),
        compiler_params=_params(1, block_bytes),
        name="ffn",
    )(x2, g, w1, w2, fg)


def _swap_pairs(x):
    lane = lax.broadcasted_iota(jnp.int32, x.shape, 1)
    return jnp.where((lane & 1) == 0, pltpu.roll(x, V7X_LANES - 1, 1), pltpu.roll(x, 1, 1))


def _mixer_kernel(x_ref, g_ref, win_ref, cos_ref, sin_ref, dmask_ref, qdec_ref, kdec_ref,
                  gn_ref, wpool_ref, pscale_ref, wout_ref, o_ref,
                  z_scr, cat_scr, state_scr, pext_scr):
    seq_tile = pl.program_id(1)

    @pl.when(seq_tile == 0)
    def _():
        state_scr[...] = jnp.zeros_like(state_scr)
        pext_scr[0:MAX_WINDOW, :] = jnp.zeros((MAX_WINDOW, POOL_WIDTH), F32)

    x = x_ref[0]
    h = _rms(x, g_ref[...]).astype(BF16)
    z_scr[...] = _dot(h, win_ref[...])

    for hd in range(RET_HEADS):
        lanes = slice(hd * RET_HEAD_DIM, (hd + 1) * RET_HEAD_DIM)
        state_decay = math.exp(RET_TILE * _ret_log_gamma(hd))
        for st in range(TOKEN_TILE // RET_TILE):
            rows = slice(st * RET_TILE, (st + 1) * RET_TILE)
            cos = cos_ref[rows, :]
            sin = sin_ref[rows, :]
            q = z_scr[rows, lanes]
            k = z_scr[rows, RET_WIDTH + hd * RET_HEAD_DIM:RET_WIDTH + (hd + 1) * RET_HEAD_DIM]
            v = z_scr[rows, 2 * RET_WIDTH + hd * RET_HEAD_DIM:2 * RET_WIDTH + (hd + 1) * RET_HEAD_DIM]
            gate = z_scr[rows, 3 * RET_WIDTH + hd * RET_HEAD_DIM:3 * RET_WIDTH + (hd + 1) * RET_HEAD_DIM]
            q = q * cos + _swap_pairs(q) * sin
            k = (k * cos + _swap_pairs(k) * sin) * (RET_HEAD_DIM ** -0.5)
            vb = v.astype(BF16)
            scores = _dot_nt(q.astype(BF16), k.astype(BF16)) * dmask_ref[hd]
            state = state_scr[hd]
            o = _dot(scores.astype(BF16), vb) + _dot((q * qdec_ref[hd]).astype(BF16), state.astype(BF16))
            state_scr[hd] = state * state_decay + _dot_tn((k * kdec_ref[hd]).astype(BF16), vb)
            mu = jnp.mean(o, axis=-1, keepdims=True)
            dev = o - mu
            var = jnp.mean(dev * dev, axis=-1, keepdims=True)
            y = dev * lax.rsqrt(var + GN_EPS) * gn_ref[:, lanes]
            cat_scr[rows, lanes] = (jax.nn.silu(gate) * y).astype(BF16)

    pext_scr[MAX_WINDOW:MAX_WINDOW + TOKEN_TILE, :] = z_scr[:, 4 * RET_WIDTH:4 * RET_WIDTH + POOL_WIDTH]
    pos = lax.broadcasted_iota(jnp.int32, (TOKEN_TILE, POOL_GROUP_DIM), 0) + seq_tile * TOKEN_TILE
    for gi, w in enumerate(POOL_WINDOWS):
        lanes = slice(gi * POOL_GROUP_DIM, (gi + 1) * POOL_GROUP_DIM)
        tok = pext_scr[MAX_WINDOW:MAX_WINDOW + TOKEN_TILE, lanes]
        total = tok
        for lag in range(1, w):
            total = total + pext_scr[MAX_WINDOW - lag:MAX_WINDOW - lag + TOKEN_TILE, lanes]
        count = jnp.minimum(pos + 1, w).astype(F32)
        pooled = total / count - tok
        mixed = _dot(pooled.astype(BF16), wpool_ref[gi]) * pscale_ref[:, lanes]
        cat_scr[:, RET_WIDTH + gi * POOL_GROUP_DIM:RET_WIDTH + (gi + 1) * POOL_GROUP_DIM] = mixed.astype(BF16)
    pext_scr[0:MAX_WINDOW, :] = pext_scr[TOKEN_TILE:TOKEN_TILE + MAX_WINDOW, :]

    o_ref[0] = x + _dot(cat_scr[...], wout_ref[...])


def _mixer(x, g, w_in, cos, sin, dmask, qdec, kdec, gn, w_pool, pscale, w_out):
    b, s, _ = x.shape
    tile = pl.BlockSpec((1, TOKEN_TILE, D_MODEL), lambda bi, si: (bi, si, 0))
    rot = pl.BlockSpec((TOKEN_TILE, RET_HEAD_DIM), lambda bi, si: (si, 0))
    block_bytes = (4 * TOKEN_TILE * D_MODEL * 4 + D_MODEL * AB_IN_WIDTH * 2 + D_MODEL * D_MODEL * 2
                   + TOKEN_TILE * AB_IN_WIDTH * 4 + 4 * RET_TILE * RET_TILE * 4)
    return pl.pallas_call(
        _mixer_kernel,
        grid=(b, s // TOKEN_TILE),
        in_specs=[
            tile, _resident((1, D_MODEL)), _resident((D_MODEL, AB_IN_WIDTH)), rot, rot,
            _resident((RET_HEADS, RET_TILE, RET_TILE)),
            _resident((RET_HEADS, RET_TILE, RET_HEAD_DIM)),
            _resident((RET_HEADS, RET_TILE, RET_HEAD_DIM)),
            _resident((1, RET_WIDTH)),
            _resident((len(POOL_WINDOWS), POOL_GROUP_DIM, POOL_GROUP_DIM)),
            _resident((1, POOL_WIDTH)), _resident((D_MODEL, D_MODEL)),
        ],
        out_specs=tile,
        out_shape=jax.ShapeDtypeStruct(x.shape, F32),
        scratch_shapes=[
            pltpu.VMEM((TOKEN_TILE, AB_IN_WIDTH), F32),
            pltpu.VMEM((TOKEN_TILE, D_MODEL), BF16),
            pltpu.VMEM((RET_HEADS, RET_HEAD_DIM, RET_HEAD_DIM), F32),
            pltpu.VMEM((TOKEN_TILE + MAX_WINDOW, POOL_WIDTH), F32),
        ],
        compiler_params=_params(2, block_bytes),
        name="mixer_even",
    )(x, g, w_in, cos, sin, dmask, qdec, kdec, gn, w_pool, pscale, w_out)


def _qkv_kernel(x_ref, g_ref, w_ref, q_ref, k_ref, v_ref):
    h = _rms(x_ref[0], g_ref[...]).astype(BF16)
    for ref, col0, scale in ((q_ref, 0, ATT_HEAD_DIM ** -0.5 * LOG2_E), (k_ref, D_MODEL, None),
                             (v_ref, 2 * D_MODEL, None)):
        z = _dot(h, w_ref[:, col0:col0 + D_MODEL])
        if scale is not None:
            z = z * scale
        for pair in range(ATT_PAIRS):
            ref[0, pair] = z[:, pair * V7X_LANES:(pair + 1) * V7X_LANES].astype(BF16)


def _qkv(x, g, w):
    b, s, _ = x.shape
    tile = pl.BlockSpec((1, QKV_TOKEN_TILE, D_MODEL), lambda bi, si: (bi, si, 0))
    pair_tile = pl.BlockSpec((1, ATT_PAIRS, QKV_TOKEN_TILE, V7X_LANES), lambda bi, si: (bi, 0, si, 0))
    out = jax.ShapeDtypeStruct((b, ATT_PAIRS, s, V7X_LANES), BF16)
    block_bytes = (2 * QKV_TOKEN_TILE * D_MODEL * 4 + 6 * QKV_TOKEN_TILE * D_MODEL * 2
                   + 3 * D_MODEL * D_MODEL * 2)
    return pl.pallas_call(
        _qkv_kernel,
        grid=(b, s // QKV_TOKEN_TILE),
        in_specs=[tile, _resident((1, D_MODEL)), _resident((D_MODEL, 3 * D_MODEL))],
        out_specs=(pair_tile, pair_tile, pair_tile),
        out_shape=(out, out, out),
        compiler_params=_params(2, block_bytes),
        name="attn_qkv",
    )(x, g, w)


def _attn_kernel(x_ref, q_ref, kp_ref, kc_ref, vp_ref, vc_ref, bias_ref, wout_ref, o_ref,
                 kcat_scr, vcat_scr, opair_scr, ocat_scr, *bufs):
    seq_tile = pl.program_id(1)

    kcat_scr[:, 0:TOKEN_TILE, :] = kp_ref[0]
    kcat_scr[:, TOKEN_TILE:2 * TOKEN_TILE, :] = kc_ref[0]
    vcat_scr[:, 0:TOKEN_TILE, :] = vp_ref[0]
    vcat_scr[:, TOKEN_TILE:2 * TOKEN_TILE, :] = vc_ref[0]

    lane = lax.broadcasted_iota(jnp.int32, (ATT_GROUP_ROWS, V7X_LANES), 1)
    first_head = lane < ATT_HEAD_DIM
    key_row = lax.broadcasted_iota(jnp.int32, (1, ATT_GROUP_BAND), 1)

    def row0(group):
        r0 = group * ATT_GROUP_ROWS
        return r0 if isinstance(group, int) else pl.multiple_of(r0, ATT_GROUP_ROWS)

    def scores(group, pair, s_ref, first_tile):
        r0 = row0(group)
        q = q_ref[0, pair, pl.ds(r0, ATT_GROUP_ROWS), :]
        zero = jnp.zeros_like(q)
        q2 = jnp.concatenate([jnp.where(first_head, q, zero), jnp.where(first_head, zero, q)], axis=0)
        s = _dot_nt(q2, kcat_scr[pair, pl.ds(r0, ATT_GROUP_BAND), :]) + bias_ref[pair]
        if first_tile:
            s = s + jnp.where(key_row + r0 >= TOKEN_TILE, 0.0, NEG_INF)
        s_ref[...] = s

    def attend(group, pair, s_ref, p_ref):
        r0 = row0(group)
        inv_denoms = []
        for rb in range(0, 2 * ATT_GROUP_ROWS, ATT_SOFTMAX_ROWS):
            s = s_ref[rb:rb + ATT_SOFTMAX_ROWS, :]
            p = jnp.exp2((s - jnp.max(s, axis=-1, keepdims=True)).astype(BF16))
            p_ref[rb:rb + ATT_SOFTMAX_ROWS, :] = p
            part = p[:, 0:V7X_LANES]
            for c in range(V7X_LANES, ATT_GROUP_BAND, V7X_LANES):
                part = part + p[:, c:c + V7X_LANES]
            inv_denoms.append(1.0 / jnp.sum(part.astype(F32), axis=-1, keepdims=True))
        pv = _dot(p_ref[...], vcat_scr[pair, pl.ds(r0, ATT_GROUP_BAND), :])
        pv = pv * jnp.concatenate(inv_denoms, axis=0)
        out = jnp.where(first_head, pv[0:ATT_GROUP_ROWS], pv[ATT_GROUP_ROWS:2 * ATT_GROUP_ROWS])
        opair_scr[pair, pl.ds(r0, ATT_GROUP_ROWS), :] = out.astype(BF16)

    def run(first_tile):
        score_scrs = bufs[:ATT_SCORE_BUFS]
        prob_scrs = bufs[ATT_SCORE_BUFS:]

        def block(group, last):
            for pair in range(ATT_PAIRS):
                ahead = pair + ATT_LOOKAHEAD
                if not (last and ahead >= ATT_PAIRS):
                    scores(group + ahead // ATT_PAIRS, ahead % ATT_PAIRS, score_scrs[ahead % ATT_SCORE_BUFS],
                           first_tile)
                attend(group, pair, score_scrs[pair % ATT_SCORE_BUFS], prob_scrs[pair % ATT_PROB_BUFS])

        def body(group, carry):
            block(group, False)
            return carry

        for pair in range(ATT_LOOKAHEAD):
            scores(0, pair, score_scrs[pair], first_tile)
        lax.fori_loop(0, ATT_GROUPS - 1, body, 0)
        block(ATT_GROUPS - 1, True)

    pl.when(seq_tile == 0)(functools.partial(run, True))
    pl.when(seq_tile != 0)(functools.partial(run, False))

    for pair in range(ATT_PAIRS):
        ocat_scr[:, pair * V7X_LANES:(pair + 1) * V7X_LANES] = opair_scr[pair]
    o_ref[0] = x_ref[0] + _dot(ocat_scr[...], wout_ref[...])


def _attn(x, q, k, v, bias, w_out):
    b, s, _ = x.shape
    tile = pl.BlockSpec((1, TOKEN_TILE, D_MODEL), lambda bi, si: (bi, si, 0))
    cur = pl.BlockSpec((1, ATT_PAIRS, TOKEN_TILE, V7X_LANES), lambda bi, si: (bi, 0, si, 0))
    prev = pl.BlockSpec((1, ATT_PAIRS, TOKEN_TILE, V7X_LANES),
                        lambda bi, si: (bi, 0, jnp.maximum(si - 1, 0), 0))
    score_buf = pltpu.VMEM((2 * ATT_GROUP_ROWS, ATT_GROUP_BAND), F32)
    prob_buf = pltpu.VMEM((2 * ATT_GROUP_ROWS, ATT_GROUP_BAND), BF16)
    block_bytes = (4 * TOKEN_TILE * D_MODEL * 4 + 10 * TOKEN_TILE * D_MODEL * 2 + D_MODEL * D_MODEL * 2
                   + ATT_HEADS * ATT_GROUP_ROWS * ATT_GROUP_BAND * 4 + 6 * TOKEN_TILE * D_MODEL * 2
                   + (4 * ATT_SCORE_BUFS + 2 * ATT_PROB_BUFS) * 2 * ATT_GROUP_ROWS * ATT_GROUP_BAND)
    return pl.pallas_call(
        _attn_kernel,
        grid=(b, s // TOKEN_TILE),
        in_specs=[
            tile, cur, prev, cur, prev, cur,
            _resident((ATT_PAIRS, 2 * ATT_GROUP_ROWS, ATT_GROUP_BAND)), _resident((D_MODEL, D_MODEL)),
        ],
        out_specs=tile,
        out_shape=jax.ShapeDtypeStruct(x.shape, F32),
        scratch_shapes=[
            pltpu.VMEM((ATT_PAIRS, 2 * TOKEN_TILE, V7X_LANES), BF16),
            pltpu.VMEM((ATT_PAIRS, 2 * TOKEN_TILE, V7X_LANES), BF16),
            pltpu.VMEM((ATT_PAIRS, TOKEN_TILE, V7X_LANES), BF16),
            pltpu.VMEM((TOKEN_TILE, D_MODEL), BF16),
        ] + [score_buf] * ATT_SCORE_BUFS + [prob_buf] * ATT_PROB_BUFS,
        compiler_params=_params(2, block_bytes),
        name="attn_band",
    )(x, q, k, k, v, v, bias, w_out)


def kernel(x, mix_norm, ffn_norm, w_ffn_in, w_ffn_out, ab_w_in, ab_gn_gain, ab_w_pool, ab_pool_scale,
           ab_w_out, c_w_qkv, c_rel_bias, c_w_out, final_norm):
    b, s, d = x.shape
    depth = mix_norm.shape[0]
    assert d == D_MODEL and s % max(TOKEN_TILE, QKV_TOKEN_TILE, FFN_TOKEN_TILE) == 0, (x.shape,)
    n = b * s

    cos, sin = _rotary_tables(s)
    dmask, qdec, kdec = _decay_tables()
    fg = final_norm.reshape(1, D_MODEL)

    for layer in range(depth):
        i = layer // 2
        g_mix = mix_norm[layer].reshape(1, D_MODEL)
        if layer % 2 == 0:
            x = _mixer(x, g_mix, ab_w_in[i].astype(BF16), cos, sin, dmask, qdec, kdec,
                       ab_gn_gain[i].reshape(1, RET_WIDTH), ab_w_pool[i].astype(BF16),
                       ab_pool_scale[i].reshape(1, POOL_WIDTH), ab_w_out[i].astype(BF16))
        else:
            q, k, v = _qkv(x, g_mix, c_w_qkv[i].astype(BF16))
            bias = _bias_table(c_rel_bias[i]).reshape(ATT_PAIRS, 2 * ATT_GROUP_ROWS, ATT_GROUP_BAND)
            x = _attn(x, q, k, v, bias, c_w_out[i].astype(BF16))
        x = _ffn(x.reshape(n, d), ffn_norm[layer].reshape(1, D_MODEL), w_ffn_in[layer].astype(BF16),
                 w_ffn_out[layer].astype(BF16), fg, final_norm=(layer == depth - 1)).reshape(b, s, d)
    return x
```

```python
import functools
import math

import jax
import jax.numpy as jnp
from jax import lax
from jax.experimental import pallas as pl
from jax.experimental.pallas import tpu as pltpu

F32 = jnp.float32
BF16 = jnp.bfloat16

D_MODEL = 1024
D_FF = 4 * D_MODEL
CHUNK = 64
RMS_EPS = 1e-6
RET_WIDTH = 512
RET_HEADS = 4
RET_HEAD_DIM = 128
RET_ROPE_BASE = 10000.0
GN_EPS = 1e-5
POOL_WIDTH = 512
POOL_WINDOWS = (2, 4, 8, 16)
POOL_GROUP_DIM = 128
MAX_WINDOW = max(POOL_WINDOWS)
AB_IN_WIDTH = 4 * RET_WIDTH + POOL_WIDTH
ATT_HEADS = 16
ATT_HEAD_DIM = 64
LEFT_CHUNKS = 8
BAND = (LEFT_CHUNKS + 1) * CHUNK
REL_CLIP = 128
N_REL = 2 * REL_CLIP + 1
NEG_INF = -1e30

V7X_LANES = 128
V7X_VMEM_BYTES = 64 * 1024 * 1024
V7X_VMEM_HEADROOM_BYTES = 8 * 1024 * 1024

TOKEN_TILE = 512
FFN_TOKEN_TILE = 1024
QKV_TOKEN_TILE = 1024
RET_TILE = 256
FFN_HIDDEN_TILE = 1024
ATT_GROUP_CHUNKS = 2
ATT_GROUP_ROWS = ATT_GROUP_CHUNKS * CHUNK
ATT_GROUP_BAND = BAND + (ATT_GROUP_CHUNKS - 1) * CHUNK
ATT_PAIRS = ATT_HEADS // 2
ATT_GROUPS = TOKEN_TILE // ATT_GROUP_ROWS
ATT_LOOKAHEAD = 3
ATT_SCORE_BUFS = 8
ATT_PROB_BUFS = 8
ATT_SOFTMAX_ROWS = 64
ATT_ONES_ROWS = 16
LOG2_E = math.log2(math.e)
REL_PAD = 3 * V7X_LANES
ROT_TABLE_TILE = 1024


def _vmem_limit(block_bytes):
    return int(min(block_bytes + 24 * 1024 * 1024, V7X_VMEM_BYTES - V7X_VMEM_HEADROOM_BYTES))


def _params(n_axes, block_bytes, flags=None):
    return pltpu.CompilerParams(
        dimension_semantics=("arbitrary",) * n_axes,
        vmem_limit_bytes=_vmem_limit(block_bytes),
        flags=flags,
    )


def _resident(shape):
    nd = len(shape)
    return pl.BlockSpec(shape, lambda *_: (0,) * nd, pipeline_mode=pl.Buffered(1))


def _rms(x, g):
    return x * lax.rsqrt(jnp.mean(x * x, axis=-1, keepdims=True) + RMS_EPS) * g


def _dot(a, b):
    return jnp.dot(a, b, preferred_element_type=F32)


def _dot_nt(a, b):
    return lax.dot_general(a, b, (((1,), (1,)), ((), ())), preferred_element_type=F32)


def _dot_tn(a, b):
    return lax.dot_general(a, b, (((0,), (0,)), ((), ())), preferred_element_type=F32)


def _rotary_table_kernel(cos_ref, sin_ref):
    rows = cos_ref.shape[0]
    n = (lax.broadcasted_iota(jnp.int32, (rows, RET_HEAD_DIM), 0) + pl.program_id(0) * rows).astype(F32)
    lane = lax.broadcasted_iota(jnp.int32, (rows, RET_HEAD_DIM), 1)
    frac = (lane >> 1).astype(F32) / float(RET_HEAD_DIM // 2 - 1)
    inv_freq = 1.0 / jnp.exp(frac * math.log(RET_ROPE_BASE))
    ang = n * inv_freq
    cos_ref[...] = jnp.cos(ang)
    sin_ref[...] = jnp.where((lane & 1) == 0, -1.0, 1.0) * jnp.sin(ang)


def _rotary_tables(seq):
    tile = min(ROT_TABLE_TILE, seq)
    shape = jax.ShapeDtypeStruct((seq, RET_HEAD_DIM), F32)
    spec = pl.BlockSpec((tile, RET_HEAD_DIM), lambda i: (i, 0))
    return pl.pallas_call(
        _rotary_table_kernel,
        grid=(seq // tile,),
        out_specs=(spec, spec),
        out_shape=(shape, shape),
        compiler_params=_params(1, 0),
        name="rotary_tables",
    )()


def _ret_log_gamma(h):
    return math.log1p(-(2.0 ** (-5.0 - h)))


def _decay_table_kernel(dmask_ref, qdec_ref, kdec_ref):
    t = lax.broadcasted_iota(jnp.int32, (RET_TILE, RET_TILE), 0)
    m = lax.broadcasted_iota(jnp.int32, (RET_TILE, RET_TILE), 1)
    dist = jnp.abs(t - m).astype(F32)
    chunk_shift = CHUNK.bit_length() - 1
    visible = (m >> chunk_shift) <= (t >> chunk_shift)
    tq = lax.broadcasted_iota(jnp.int32, (RET_TILE, RET_HEAD_DIM), 0).astype(F32)
    for h in range(RET_HEADS):
        lg = _ret_log_gamma(h)
        dmask_ref[h] = jnp.where(visible, jnp.exp(dist * lg), 0.0)
        qdec_ref[h] = jnp.exp((tq + 1.0) * lg)
        kdec_ref[h] = jnp.exp((RET_TILE - 1.0 - tq) * lg)


def _decay_tables():
    return pl.pallas_call(
        _decay_table_kernel,
        out_shape=(
            jax.ShapeDtypeStruct((RET_HEADS, RET_TILE, RET_TILE), F32),
            jax.ShapeDtypeStruct((RET_HEADS, RET_TILE, RET_HEAD_DIM), F32),
            jax.ShapeDtypeStruct((RET_HEADS, RET_TILE, RET_HEAD_DIM), F32),
        ),
        name="decay_tables",
    )()


_BIAS_ROW_TILE = 8
_BIAS_LEFT_PAD = CHUNK
_BIAS_EXT = 7 * V7X_LANES
_BIAS_WIDE = 6 * V7X_LANES


def _split3(x):
    hi = x.astype(BF16)
    r1 = x - hi.astype(F32)
    mid = r1.astype(BF16)
    lo = (r1 - mid.astype(F32)).astype(BF16)
    return hi, mid, lo


def _bias_table_kernel(rel_ref, out_ref):
    r = lax.broadcasted_iota(jnp.int32, (REL_PAD, _BIAS_EXT), 0)
    u = lax.broadcasted_iota(jnp.int32, (REL_PAD, _BIAS_EXT), 1)
    d = u - _BIAS_LEFT_PAD
    rel_idx = jnp.minimum(LEFT_CHUNKS * CHUNK + (CHUNK - 1) - d, REL_CLIP) + REL_CLIP
    inside = (d >= 0) & (d < BAND + CHUNK - 1)
    sel = jnp.where(inside & (r == rel_idx), 1.0, 0.0).astype(BF16)
    hi, mid, lo = _split3(rel_ref[...])
    diag = _dot(hi, sel) + _dot(mid, sel) + _dot(lo, sel)

    sub = lax.broadcasted_iota(jnp.int32, (_BIAS_ROW_TILE, _BIAS_WIDE), 0)
    col = lax.broadcasted_iota(jnp.int32, (_BIAS_ROW_TILE, ATT_GROUP_BAND), 1)
    for h in range(ATT_HEADS):
        row = diag[h:h + 1, :]
        skew = jnp.zeros((_BIAS_ROW_TILE, _BIAS_WIDE), F32)
        for b in range(_BIAS_ROW_TILE):
            start = _BIAS_ROW_TILE - 1 - b
            piece = jnp.broadcast_to(row[:, start:start + _BIAS_WIDE], (_BIAS_ROW_TILE, _BIAS_WIDE))
            skew = jnp.where(sub == b, piece, skew)
        for c in range(ATT_GROUP_CHUNKS):
            visible = (col >= c * CHUNK) & (col < c * CHUNK + BAND)
            for a in range(CHUNK // _BIAS_ROW_TILE):
                start = _BIAS_LEFT_PAD + (CHUNK - _BIAS_ROW_TILE) - c * CHUNK - _BIAS_ROW_TILE * a
                tile = skew[:, start:start + ATT_GROUP_BAND]
                r0 = c * CHUNK + _BIAS_ROW_TILE * a
                out_ref[h, r0:r0 + _BIAS_ROW_TILE, :] = jnp.where(visible, tile * LOG2_E, NEG_INF)


def _bias_table(rel_bias):
    rel = jnp.pad(rel_bias, ((0, 0), (0, REL_PAD - N_REL)))
    return pl.pallas_call(
        _bias_table_kernel,
        out_shape=jax.ShapeDtypeStruct((ATT_HEADS, ATT_GROUP_ROWS, ATT_GROUP_BAND), F32),
        name="bias_table",
    )(rel)


def _ffn_kernel(x_ref, g_ref, w1_ref, w2_ref, fg_ref, o_ref, *, final_norm):
    x = x_ref[...]
    h = _rms(x, g_ref[...]).astype(BF16)
    acc = x
    for j in range(D_FF // FFN_HIDDEN_TILE):
        cols = slice(j * FFN_HIDDEN_TILE, (j + 1) * FFN_HIDDEN_TILE)
        a = jnp.maximum(_dot(h, w1_ref[:, cols]), 0.0)
        acc = acc + _dot((a * a).astype(BF16), w2_ref[cols, :])
    if final_norm:
        acc = _rms(acc, fg_ref[...])
    o_ref[...] = acc


def _ffn(x2, g, w1, w2, fg, final_norm):
    n = x2.shape[0]
    tile = pl.BlockSpec((FFN_TOKEN_TILE, D_MODEL), lambda i: (i, 0))
    vec = _resident((1, D_MODEL))
    block_bytes = 4 * FFN_TOKEN_TILE * D_MODEL * 4 + 2 * D_MODEL * D_FF * 2
    return pl.pallas_call(
        functools.partial(_ffn_kernel, final_norm=final_norm),
        grid=(n // FFN_TOKEN_TILE,),
        in_specs=[tile, vec, _resident((D_MODEL, D_FF)), _resident((D_FF, D_MODEL)), vec],
        out_specs=tile,
        out_shape=jax.ShapeDtypeStruct((n, D_MODEL), F32),
        compiler_params=_params(1, block_bytes),
        name="ffn",
    )(x2, g, w1, w2, fg)


def _swap_pairs(x):
    lane = lax.broadcasted_iota(jnp.int32, x.shape, 1)
    return jnp.where((lane & 1) == 0, pltpu.roll(x, V7X_LANES - 1, 1), pltpu.roll(x, 1, 1))


def _mixer_kernel(x_ref, g_ref, win_ref, cos_ref, sin_ref, dmask_ref, qdec_ref, kdec_ref,
                  gn_ref, wpool_ref, pscale_ref, wout_ref, o_ref,
                  z_scr, cat_scr, state_scr, pext_scr):
    seq_tile = pl.program_id(1)

    @pl.when(seq_tile == 0)
    def _():
        state_scr[...] = jnp.zeros_like(state_scr)
        pext_scr[0:MAX_WINDOW, :] = jnp.zeros((MAX_WINDOW, POOL_WIDTH), F32)

    x = x_ref[0]
    h = _rms(x, g_ref[...]).astype(BF16)
    z_scr[...] = _dot(h, win_ref[...])

    for hd in range(RET_HEADS):
        lanes = slice(hd * RET_HEAD_DIM, (hd + 1) * RET_HEAD_DIM)
        state_decay = math.exp(RET_TILE * _ret_log_gamma(hd))
        for st in range(TOKEN_TILE // RET_TILE):
            rows = slice(st * RET_TILE, (st + 1) * RET_TILE)
            cos = cos_ref[rows, :]
            sin = sin_ref[rows, :]
            q = z_scr[rows, lanes]
            k = z_scr[rows, RET_WIDTH + hd * RET_HEAD_DIM:RET_WIDTH + (hd + 1) * RET_HEAD_DIM]
            v = z_scr[rows, 2 * RET_WIDTH + hd * RET_HEAD_DIM:2 * RET_WIDTH + (hd + 1) * RET_HEAD_DIM]
            gate = z_scr[rows, 3 * RET_WIDTH + hd * RET_HEAD_DIM:3 * RET_WIDTH + (hd + 1) * RET_HEAD_DIM]
            q = q * cos + _swap_pairs(q) * sin
            k = (k * cos + _swap_pairs(k) * sin) * (RET_HEAD_DIM ** -0.5)
            vb = v.astype(BF16)
            scores = _dot_nt(q.astype(BF16), k.astype(BF16)) * dmask_ref[hd]
            state = state_scr[hd]
            o = _dot(scores.astype(BF16), vb) + _dot((q * qdec_ref[hd]).astype(BF16), state.astype(BF16))
            state_scr[hd] = state * state_decay + _dot_tn((k * kdec_ref[hd]).astype(BF16), vb)
            mu = jnp.mean(o, axis=-1, keepdims=True)
            dev = o - mu
            var = jnp.mean(dev * dev, axis=-1, keepdims=True)
            y = dev * lax.rsqrt(var + GN_EPS) * gn_ref[:, lanes]
            cat_scr[rows, lanes] = (jax.nn.silu(gate) * y).astype(BF16)

    pext_scr[MAX_WINDOW:MAX_WINDOW + TOKEN_TILE, :] = z_scr[:, 4 * RET_WIDTH:4 * RET_WIDTH + POOL_WIDTH]
    pos = lax.broadcasted_iota(jnp.int32, (TOKEN_TILE, POOL_GROUP_DIM), 0) + seq_tile * TOKEN_TILE
    for gi, w in enumerate(POOL_WINDOWS):
        lanes = slice(gi * POOL_GROUP_DIM, (gi + 1) * POOL_GROUP_DIM)
        tok = pext_scr[MAX_WINDOW:MAX_WINDOW + TOKEN_TILE, lanes]
        total = tok
        for lag in range(1, w):
            total = total + pext_scr[MAX_WINDOW - lag:MAX_WINDOW - lag + TOKEN_TILE, lanes]
        count = jnp.minimum(pos + 1, w).astype(F32)
        pooled = total / count - tok
        mixed = _dot(pooled.astype(BF16), wpool_ref[gi]) * pscale_ref[:, lanes]
        cat_scr[:, RET_WIDTH + gi * POOL_GROUP_DIM:RET_WIDTH + (gi + 1) * POOL_GROUP_DIM] = mixed.astype(BF16)
    pext_scr[0:MAX_WINDOW, :] = pext_scr[TOKEN_TILE:TOKEN_TILE + MAX_WINDOW, :]

    o_ref[0] = x + _dot(cat_scr[...], wout_ref[...])


def _mixer(x, g, w_in, cos, sin, dmask, qdec, kdec, gn, w_pool, pscale, w_out):
    b, s, _ = x.shape
    tile = pl.BlockSpec((1, TOKEN_TILE, D_MODEL), lambda bi, si: (bi, si, 0))
    rot = pl.BlockSpec((TOKEN_TILE, RET_HEAD_DIM), lambda bi, si: (si, 0))
    block_bytes = (4 * TOKEN_TILE * D_MODEL * 4 + D_MODEL * AB_IN_WIDTH * 2 + D_MODEL * D_MODEL * 2
                   + TOKEN_TILE * AB_IN_WIDTH * 4 + 4 * RET_TILE * RET_TILE * 4)
    return pl.pallas_call(
        _mixer_kernel,
        grid=(b, s // TOKEN_TILE),
        in_specs=[
            tile, _resident((1, D_MODEL)), _resident((D_MODEL, AB_IN_WIDTH)), rot, rot,
            _resident((RET_HEADS, RET_TILE, RET_TILE)),
            _resident((RET_HEADS, RET_TILE, RET_HEAD_DIM)),
            _resident((RET_HEADS, RET_TILE, RET_HEAD_DIM)),
            _resident((1, RET_WIDTH)),
            _resident((len(POOL_WINDOWS), POOL_GROUP_DIM, POOL_GROUP_DIM)),
            _resident((1, POOL_WIDTH)), _resident((D_MODEL, D_MODEL)),
        ],
        out_specs=tile,
        out_shape=jax.ShapeDtypeStruct(x.shape, F32),
        scratch_shapes=[
            pltpu.VMEM((TOKEN_TILE, AB_IN_WIDTH), F32),
            pltpu.VMEM((TOKEN_TILE, D_MODEL), BF16),
            pltpu.VMEM((RET_HEADS, RET_HEAD_DIM, RET_HEAD_DIM), F32),
            pltpu.VMEM((TOKEN_TILE + MAX_WINDOW, POOL_WIDTH), F32),
        ],
        compiler_params=_params(2, block_bytes),
        name="mixer_even",
    )(x, g, w_in, cos, sin, dmask, qdec, kdec, gn, w_pool, pscale, w_out)


def _qkv_kernel(x_ref, g_ref, w_ref, q_ref, k_ref, v_ref):
    h = _rms(x_ref[0], g_ref[...]).astype(BF16)
    for ref, col0, scale in ((q_ref, 0, ATT_HEAD_DIM ** -0.5 * LOG2_E), (k_ref, D_MODEL, None),
                             (v_ref, 2 * D_MODEL, None)):
        z = _dot(h, w_ref[:, col0:col0 + D_MODEL])
        if scale is not None:
            z = z * scale
        for pair in range(ATT_PAIRS):
            ref[0, pair] = z[:, pair * V7X_LANES:(pair + 1) * V7X_LANES].astype(BF16)


def _qkv(x, g, w):
    b, s, _ = x.shape
    tile = pl.BlockSpec((1, QKV_TOKEN_TILE, D_MODEL), lambda bi, si: (bi, si, 0))
    pair_tile = pl.BlockSpec((1, ATT_PAIRS, QKV_TOKEN_TILE, V7X_LANES), lambda bi, si: (bi, 0, si, 0))
    out = jax.ShapeDtypeStruct((b, ATT_PAIRS, s, V7X_LANES), BF16)
    block_bytes = (2 * QKV_TOKEN_TILE * D_MODEL * 4 + 6 * QKV_TOKEN_TILE * D_MODEL * 2
                   + 3 * D_MODEL * D_MODEL * 2)
    return pl.pallas_call(
        _qkv_kernel,
        grid=(b, s // QKV_TOKEN_TILE),
        in_specs=[tile, _resident((1, D_MODEL)), _resident((D_MODEL, 3 * D_MODEL))],
        out_specs=(pair_tile, pair_tile, pair_tile),
        out_shape=(out, out, out),
        compiler_params=_params(2, block_bytes),
        name="attn_qkv",
    )(x, g, w)


def _attn_kernel(x_ref, q_ref, kp_ref, kc_ref, vp_ref, vc_ref, bias_ref, wout_ref, o_ref,
                 kcat_scr, vcat_scr, opair_scr, *bufs):
    seq_tile = pl.program_id(1)

    kcat_scr[:, 0:TOKEN_TILE, :] = kp_ref[0]
    kcat_scr[:, TOKEN_TILE:2 * TOKEN_TILE, :] = kc_ref[0]
    vcat_scr[:, 0:TOKEN_TILE, :] = vp_ref[0]
    vcat_scr[:, TOKEN_TILE:2 * TOKEN_TILE, :] = vc_ref[0]

    lane = lax.broadcasted_iota(jnp.int32, (ATT_GROUP_ROWS, V7X_LANES), 1)
    first_head = lane < ATT_HEAD_DIM
    head_dim_row = lax.broadcasted_iota(jnp.int32, (V7X_LANES, ATT_GROUP_ROWS), 0) < ATT_HEAD_DIM
    key_row = lax.broadcasted_iota(jnp.int32, (ATT_GROUP_BAND, 2 * ATT_GROUP_ROWS), 0)
    ones_rows = jnp.ones((ATT_ONES_ROWS, ATT_GROUP_BAND), BF16)

    def row0(group):
        r0 = group * ATT_GROUP_ROWS
        return r0 if isinstance(group, int) else pl.multiple_of(r0, ATT_GROUP_ROWS)

    def scores(group, pair, s_ref, first_tile):
        r0 = row0(group)
        q = q_ref[0, pair, pl.ds(r0, ATT_GROUP_ROWS), :]
        zero = jnp.zeros_like(q)
        q2 = jnp.concatenate([jnp.where(first_head, q, zero), jnp.where(first_head, zero, q)], axis=0)
        s = _dot_nt(kcat_scr[pair, pl.ds(r0, ATT_GROUP_BAND), :], q2) + bias_ref[pair]
        if first_tile:
            s = s + jnp.where(key_row + r0 >= TOKEN_TILE, 0.0, NEG_INF)
        s_ref[...] = s

    def attend(group, pair, s_ref, p_ref):
        r0 = row0(group)
        m = jnp.max(s_ref[...], axis=0, keepdims=True)
        for rb in range(0, ATT_GROUP_BAND, ATT_SOFTMAX_ROWS):
            p_ref[rb:rb + ATT_SOFTMAX_ROWS, :] = jnp.exp2((s_ref[rb:rb + ATT_SOFTMAX_ROWS, :] - m).astype(BF16))
        v_t = vcat_scr[pair, pl.ds(r0, ATT_GROUP_BAND), :].T
        pv = _dot(jnp.concatenate([v_t, ones_rows], axis=0), p_ref[...])
        pv = pv[0:V7X_LANES, :] * (1.0 / pv[V7X_LANES:V7X_LANES + 1, :])
        out = jnp.where(head_dim_row, pv[:, 0:ATT_GROUP_ROWS], pv[:, ATT_GROUP_ROWS:2 * ATT_GROUP_ROWS])
        opair_scr[group, pair] = out.astype(BF16)

    def run(first_tile):
        score_scrs = bufs[:ATT_SCORE_BUFS]
        prob_scrs = bufs[ATT_SCORE_BUFS:]

        def block(group, last):
            for pair in range(ATT_PAIRS):
                ahead = pair + ATT_LOOKAHEAD
                if not (last and ahead >= ATT_PAIRS):
                    scores(group + ahead // ATT_PAIRS, ahead % ATT_PAIRS, score_scrs[ahead % ATT_SCORE_BUFS],
                           first_tile)
                attend(group, pair, score_scrs[pair % ATT_SCORE_BUFS], prob_scrs[pair % ATT_PROB_BUFS])

        def body(group, carry):
            block(group, False)
            return carry

        for pair in range(ATT_LOOKAHEAD):
            scores(0, pair, score_scrs[pair], first_tile)
        lax.fori_loop(0, ATT_GROUPS - 1, body, 0)
        block(ATT_GROUPS - 1, True)

    pl.when(seq_tile == 0)(functools.partial(run, True))
    pl.when(seq_tile != 0)(functools.partial(run, False))

    for group in range(ATT_GROUPS):
        rows = slice(group * ATT_GROUP_ROWS, (group + 1) * ATT_GROUP_ROWS)
        o_t = opair_scr[group].reshape(D_MODEL, ATT_GROUP_ROWS)
        o_ref[0, rows, :] = x_ref[0, rows, :] + _dot_tn(o_t, wout_ref[...])


def _attn(x, q, k, v, bias, w_out):
    b, s, _ = x.shape
    tile = pl.BlockSpec((1, TOKEN_TILE, D_MODEL), lambda bi, si: (bi, si, 0))
    cur = pl.BlockSpec((1, ATT_PAIRS, TOKEN_TILE, V7X_LANES), lambda bi, si: (bi, 0, si, 0))
    prev = pl.BlockSpec((1, ATT_PAIRS, TOKEN_TILE, V7X_LANES),
                        lambda bi, si: (bi, 0, jnp.maximum(si - 1, 0), 0))
    score_buf = pltpu.VMEM((ATT_GROUP_BAND, 2 * ATT_GROUP_ROWS), F32)
    prob_buf = pltpu.VMEM((ATT_GROUP_BAND, 2 * ATT_GROUP_ROWS), BF16)
    block_bytes = (4 * TOKEN_TILE * D_MODEL * 4 + 10 * TOKEN_TILE * D_MODEL * 2 + D_MODEL * D_MODEL * 2
                   + ATT_HEADS * ATT_GROUP_ROWS * ATT_GROUP_BAND * 4 + 6 * TOKEN_TILE * D_MODEL * 2
                   + (4 * ATT_SCORE_BUFS + 2 * ATT_PROB_BUFS) * 2 * ATT_GROUP_ROWS * ATT_GROUP_BAND)
    return pl.pallas_call(
        _attn_kernel,
        grid=(b, s // TOKEN_TILE),
        in_specs=[
            tile, cur, prev, cur, prev, cur,
            _resident((ATT_PAIRS, ATT_GROUP_BAND, 2 * ATT_GROUP_ROWS)), _resident((D_MODEL, D_MODEL)),
        ],
        out_specs=tile,
        out_shape=jax.ShapeDtypeStruct(x.shape, F32),
        scratch_shapes=[
            pltpu.VMEM((ATT_PAIRS, 2 * TOKEN_TILE, V7X_LANES), BF16),
            pltpu.VMEM((ATT_PAIRS, 2 * TOKEN_TILE, V7X_LANES), BF16),
            pltpu.VMEM((ATT_GROUPS, ATT_PAIRS, V7X_LANES, ATT_GROUP_ROWS), BF16),
        ] + [score_buf] * ATT_SCORE_BUFS + [prob_buf] * ATT_PROB_BUFS,
        compiler_params=_params(2, block_bytes),
        name="attn_band",
    )(x, q, k, k, v, v, bias, w_out)


def kernel(x, mix_norm, ffn_norm, w_ffn_in, w_ffn_out, ab_w_in, ab_gn_gain, ab_w_pool, ab_pool_scale,
           ab_w_out, c_w_qkv, c_rel_bias, c_w_out, final_norm):
    b, s, d = x.shape
    depth = mix_norm.shape[0]
    assert d == D_MODEL and s % max(TOKEN_TILE, QKV_TOKEN_TILE, FFN_TOKEN_TILE) == 0, (x.shape,)
    n = b * s

    cos, sin = _rotary_tables(s)
    dmask, qdec, kdec = _decay_tables()
    fg = final_norm.reshape(1, D_MODEL)

    for layer in range(depth):
        i = layer // 2
        g_mix = mix_norm[layer].reshape(1, D_MODEL)
        if layer % 2 == 0:
            x = _mixer(x, g_mix, ab_w_in[i].astype(BF16), cos, sin, dmask, qdec, kdec,
                       ab_gn_gain[i].reshape(1, RET_WIDTH), ab_w_pool[i].astype(BF16),
                       ab_pool_scale[i].reshape(1, POOL_WIDTH), ab_w_out[i].astype(BF16))
        else:
            q, k, v = _qkv(x, g_mix, c_w_qkv[i].astype(BF16))
            bias = _bias_table(c_rel_bias[i]).reshape(ATT_PAIRS, 2 * ATT_GROUP_ROWS, ATT_GROUP_BAND)
            bias = jnp.transpose(bias, (0, 2, 1))
            x = _attn(x, q, k, v, bias, c_w_out[i].astype(BF16))
        x = _ffn(x.reshape(n, d), ffn_norm[layer].reshape(1, D_MODEL), w_ffn_in[layer].astype(BF16),
                 w_ffn_out[layer].astype(BF16), fg, final_norm=(layer == depth - 1)).reshape(b, s, d)
    return x
```

```python
import functools
import math

import jax
import jax.numpy as jnp
from jax import lax
from jax.experimental import pallas as pl
from jax.experimental.pallas import tpu as pltpu

F32 = jnp.float32
BF16 = jnp.bfloat16

D_MODEL = 1024
D_FF = 4 * D_MODEL
CHUNK = 64
RMS_EPS = 1e-6
RET_WIDTH = 512
RET_HEADS = 4
RET_HEAD_DIM = 128
RET_ROPE_BASE = 10000.0
GN_EPS = 1e-5
POOL_WIDTH = 512
POOL_WINDOWS = (2, 4, 8, 16)
POOL_GROUP_DIM = 128
MAX_WINDOW = max(POOL_WINDOWS)
AB_IN_WIDTH = 4 * RET_WIDTH + POOL_WIDTH
ATT_HEADS = 16
ATT_HEAD_DIM = 64
LEFT_CHUNKS = 8
BAND = (LEFT_CHUNKS + 1) * CHUNK
REL_CLIP = 128
N_REL = 2 * REL_CLIP + 1
NEG_INF = -1e30

V7X_LANES = 128
V7X_VMEM_BYTES = 64 * 1024 * 1024
V7X_VMEM_HEADROOM_BYTES = 8 * 1024 * 1024

TOKEN_TILE = 512
FFN_TOKEN_TILE = 1024
QKV_TOKEN_TILE = 1024
RET_TILE = 256
FFN_HIDDEN_TILE = 1024
ATT_GROUP_CHUNKS = 2
ATT_GROUP_ROWS = ATT_GROUP_CHUNKS * CHUNK
ATT_GROUP_BAND = BAND + (ATT_GROUP_CHUNKS - 1) * CHUNK
ATT_PAIRS = ATT_HEADS // 2
ATT_GROUPS = TOKEN_TILE // ATT_GROUP_ROWS
ATT_LOOKAHEAD = 3
ATT_SCORE_BUFS = 8
ATT_PROB_BUFS = 8
ATT_SOFTMAX_ROWS = 64
ATT_ONES_ROWS = 16
LOG2_E = math.log2(math.e)
REL_PAD = 3 * V7X_LANES
ROT_TABLE_TILE = 1024


def _vmem_limit(block_bytes):
    return int(min(block_bytes + 24 * 1024 * 1024, V7X_VMEM_BYTES - V7X_VMEM_HEADROOM_BYTES))


def _params(n_axes, block_bytes, flags=None):
    return pltpu.CompilerParams(
        dimension_semantics=("arbitrary",) * n_axes,
        vmem_limit_bytes=_vmem_limit(block_bytes),
        flags=flags,
    )


def _resident(shape):
    nd = len(shape)
    return pl.BlockSpec(shape, lambda *_: (0,) * nd, pipeline_mode=pl.Buffered(1))


def _rms(x, g):
    return x * lax.rsqrt(jnp.mean(x * x, axis=-1, keepdims=True) + RMS_EPS) * g


def _dot(a, b):
    return jnp.dot(a, b, preferred_element_type=F32)


def _dot_nt(a, b):
    return lax.dot_general(a, b, (((1,), (1,)), ((), ())), preferred_element_type=F32)


def _dot_tn(a, b):
    return lax.dot_general(a, b, (((0,), (0,)), ((), ())), preferred_element_type=F32)


def _rotary_table_kernel(cos_ref, sin_ref):
    rows = cos_ref.shape[0]
    n = (lax.broadcasted_iota(jnp.int32, (rows, RET_HEAD_DIM), 0) + pl.program_id(0) * rows).astype(F32)
    lane = lax.broadcasted_iota(jnp.int32, (rows, RET_HEAD_DIM), 1)
    frac = (lane >> 1).astype(F32) / float(RET_HEAD_DIM // 2 - 1)
    inv_freq = 1.0 / jnp.exp(frac * math.log(RET_ROPE_BASE))
    ang = n * inv_freq
    cos_ref[...] = jnp.cos(ang)
    sin_ref[...] = jnp.where((lane & 1) == 0, -1.0, 1.0) * jnp.sin(ang)


def _rotary_tables(seq):
    tile = min(ROT_TABLE_TILE, seq)
    shape = jax.ShapeDtypeStruct((seq, RET_HEAD_DIM), F32)
    spec = pl.BlockSpec((tile, RET_HEAD_DIM), lambda i: (i, 0))
    return pl.pallas_call(
        _rotary_table_kernel,
        grid=(seq // tile,),
        out_specs=(spec, spec),
        out_shape=(shape, shape),
        compiler_params=_params(1, 0),
        name="rotary_tables",
    )()


def _ret_log_gamma(h):
    return math.log1p(-(2.0 ** (-5.0 - h)))


def _decay_table_kernel(dmask_ref, qdec_ref, kdec_ref):
    t = lax.broadcasted_iota(jnp.int32, (RET_TILE, RET_TILE), 0)
    m = lax.broadcasted_iota(jnp.int32, (RET_TILE, RET_TILE), 1)
    dist = jnp.abs(t - m).astype(F32)
    chunk_shift = CHUNK.bit_length() - 1
    visible = (m >> chunk_shift) <= (t >> chunk_shift)
    tq = lax.broadcasted_iota(jnp.int32, (RET_TILE, RET_HEAD_DIM), 0).astype(F32)
    for h in range(RET_HEADS):
        lg = _ret_log_gamma(h)
        dmask_ref[h] = jnp.where(visible, jnp.exp(dist * lg), 0.0)
        qdec_ref[h] = jnp.exp((tq + 1.0) * lg)
        kdec_ref[h] = jnp.exp((RET_TILE - 1.0 - tq) * lg)


def _decay_tables():
    return pl.pallas_call(
        _decay_table_kernel,
        out_shape=(
            jax.ShapeDtypeStruct((RET_HEADS, RET_TILE, RET_TILE), F32),
            jax.ShapeDtypeStruct((RET_HEADS, RET_TILE, RET_HEAD_DIM), F32),
            jax.ShapeDtypeStruct((RET_HEADS, RET_TILE, RET_HEAD_DIM), F32),
        ),
        name="decay_tables",
    )()


_BIAS_ROW_TILE = 8
_BIAS_LEFT_PAD = CHUNK
_BIAS_EXT = 7 * V7X_LANES
_BIAS_WIDE = 6 * V7X_LANES


def _split3(x):
    hi = x.astype(BF16)
    r1 = x - hi.astype(F32)
    mid = r1.astype(BF16)
    lo = (r1 - mid.astype(F32)).astype(BF16)
    return hi, mid, lo


def _bias_table_kernel(rel_ref, out_ref):
    r = lax.broadcasted_iota(jnp.int32, (REL_PAD, _BIAS_EXT), 0)
    u = lax.broadcasted_iota(jnp.int32, (REL_PAD, _BIAS_EXT), 1)
    d = u - _BIAS_LEFT_PAD
    rel_idx = jnp.minimum(LEFT_CHUNKS * CHUNK + (CHUNK - 1) - d, REL_CLIP) + REL_CLIP
    inside = (d >= 0) & (d < BAND + CHUNK - 1)
    sel = jnp.where(inside & (r == rel_idx), 1.0, 0.0).astype(BF16)
    hi, mid, lo = _split3(rel_ref[...])
    diag = _dot(hi, sel) + _dot(mid, sel) + _dot(lo, sel)

    sub = lax.broadcasted_iota(jnp.int32, (_BIAS_ROW_TILE, _BIAS_WIDE), 0)
    col = lax.broadcasted_iota(jnp.int32, (_BIAS_ROW_TILE, ATT_GROUP_BAND), 1)
    for h in range(ATT_HEADS):
        row = diag[h:h + 1, :]
        skew = jnp.zeros((_BIAS_ROW_TILE, _BIAS_WIDE), F32)
        for b in range(_BIAS_ROW_TILE):
            start = _BIAS_ROW_TILE - 1 - b
            piece = jnp.broadcast_to(row[:, start:start + _BIAS_WIDE], (_BIAS_ROW_TILE, _BIAS_WIDE))
            skew = jnp.where(sub == b, piece, skew)
        for c in range(ATT_GROUP_CHUNKS):
            visible = (col >= c * CHUNK) & (col < c * CHUNK + BAND)
            for a in range(CHUNK // _BIAS_ROW_TILE):
                start = _BIAS_LEFT_PAD + (CHUNK - _BIAS_ROW_TILE) - c * CHUNK - _BIAS_ROW_TILE * a
                tile = skew[:, start:start + ATT_GROUP_BAND]
                r0 = c * CHUNK + _BIAS_ROW_TILE * a
                out_ref[h, r0:r0 + _BIAS_ROW_TILE, :] = jnp.where(visible, tile * LOG2_E, NEG_INF)


def _bias_table(rel_bias):
    rel = jnp.pad(rel_bias, ((0, 0), (0, REL_PAD - N_REL)))
    return pl.pallas_call(
        _bias_table_kernel,
        out_shape=jax.ShapeDtypeStruct((ATT_HEADS, ATT_GROUP_ROWS, ATT_GROUP_BAND), F32),
        name="bias_table",
    )(rel)


def _ffn_kernel(x_ref, g_ref, w1_ref, w2_ref, fg_ref, o_ref, *, final_norm):
    x = x_ref[...]
    h = _rms(x, g_ref[...]).astype(BF16)
    acc = x
    for j in range(D_FF // FFN_HIDDEN_TILE):
        cols = slice(j * FFN_HIDDEN_TILE, (j + 1) * FFN_HIDDEN_TILE)
        a = jnp.maximum(_dot(h, w1_ref[:, cols]), 0.0)
        acc = acc + _dot((a * a).astype(BF16), w2_ref[cols, :])
    if final_norm:
        acc = _rms(acc, fg_ref[...])
    o_ref[...] = acc


def _ffn(x2, g, w1, w2, fg, final_norm):
    n = x2.shape[0]
    tile = pl.BlockSpec((FFN_TOKEN_TILE, D_MODEL), lambda i: (i, 0))
    vec = _resident((1, D_MODEL))
    block_bytes = 4 * FFN_TOKEN_TILE * D_MODEL * 4 + 2 * D_MODEL * D_FF * 2
    return pl.pallas_call(
        functools.partial(_ffn_kernel, final_norm=final_norm),
        grid=(n // FFN_TOKEN_TILE,),
        in_specs=[tile, vec, _resident((D_MODEL, D_FF)), _resident((D_FF, D_MODEL)), vec],
        out_specs=tile,
        out_shape=jax.ShapeDtypeStruct((n, D_MODEL), F32),
        compiler_params=_params(1, block_bytes),
        name="ffn",
    )(x2, g, w1, w2, fg)


def _swap_pairs(x):
    lane = lax.broadcasted_iota(jnp.int32, x.shape, 1)
    return jnp.where((lane & 1) == 0, pltpu.roll(x, V7X_LANES - 1, 1), pltpu.roll(x, 1, 1))


def _mixer_kernel(x_ref, g_ref, win_ref, cos_ref, sin_ref, dmask_ref, qdec_ref, kdec_ref,
                  gn_ref, wpool_ref, pscale_ref, wout_ref, o_ref,
                  z_scr, cat_scr, state_scr, pext_scr):
    seq_tile = pl.program_id(1)

    @pl.when(seq_tile == 0)
    def _():
        state_scr[...] = jnp.zeros_like(state_scr)
        pext_scr[0:MAX_WINDOW, :] = jnp.zeros((MAX_WINDOW, POOL_WIDTH), F32)

    x = x_ref[0]
    h = _rms(x, g_ref[...]).astype(BF16)
    z_scr[...] = _dot(h, win_ref[...])

    for hd in range(RET_HEADS):
        lanes = slice(hd * RET_HEAD_DIM, (hd + 1) * RET_HEAD_DIM)
        state_decay = math.exp(RET_TILE * _ret_log_gamma(hd))
        for st in range(TOKEN_TILE // RET_TILE):
            rows = slice(st * RET_TILE, (st + 1) * RET_TILE)
            cos = cos_ref[rows, :]
            sin = sin_ref[rows, :]
            q = z_scr[rows, lanes]
            k = z_scr[rows, RET_WIDTH + hd * RET_HEAD_DIM:RET_WIDTH + (hd + 1) * RET_HEAD_DIM]
            v = z_scr[rows, 2 * RET_WIDTH + hd * RET_HEAD_DIM:2 * RET_WIDTH + (hd + 1) * RET_HEAD_DIM]
            gate = z_scr[rows, 3 * RET_WIDTH + hd * RET_HEAD_DIM:3 * RET_WIDTH + (hd + 1) * RET_HEAD_DIM]
            q = q * cos + _swap_pairs(q) * sin
            k = (k * cos + _swap_pairs(k) * sin) * (RET_HEAD_DIM ** -0.5)
            vb = v.astype(BF16)
            scores = _dot_nt(q.astype(BF16), k.astype(BF16)) * dmask_ref[hd]
            state = state_scr[hd]
            o = _dot(scores.astype(BF16), vb) + _dot((q * qdec_ref[hd]).astype(BF16), state.astype(BF16))
            state_scr[hd] = state * state_decay + _dot_tn((k * kdec_ref[hd]).astype(BF16), vb)
            mu = jnp.mean(o, axis=-1, keepdims=True)
            dev = o - mu
            var = jnp.mean(dev * dev, axis=-1, keepdims=True)
            y = dev * lax.rsqrt(var + GN_EPS) * gn_ref[:, lanes]
            cat_scr[rows, lanes] = (jax.nn.silu(gate) * y).astype(BF16)

    pext_scr[MAX_WINDOW:MAX_WINDOW + TOKEN_TILE, :] = z_scr[:, 4 * RET_WIDTH:4 * RET_WIDTH + POOL_WIDTH]
    pos_head = lax.broadcasted_iota(jnp.int32, (MAX_WINDOW, POOL_GROUP_DIM), 0) + seq_tile * TOKEN_TILE
    for gi, w in enumerate(POOL_WINDOWS):
        lanes = slice(gi * POOL_GROUP_DIM, (gi + 1) * POOL_GROUP_DIM)
        ext = pext_scr[:, lanes]
        total = ext
        span = 1
        while span < w:
            total = total + pltpu.roll(total, span, 0)
            span *= 2
        total = total[MAX_WINDOW:, :]
        tok = ext[MAX_WINDOW:, :]
        inv_count = jnp.concatenate(
            [1.0 / jnp.minimum(pos_head + 1, w).astype(F32),
             jnp.full((TOKEN_TILE - MAX_WINDOW, POOL_GROUP_DIM), 1.0 / w, F32)], axis=0)
        pooled = total * inv_count - tok
        mixed = _dot(pooled.astype(BF16), wpool_ref[gi]) * pscale_ref[:, lanes]
        cat_scr[:, RET_WIDTH + gi * POOL_GROUP_DIM:RET_WIDTH + (gi + 1) * POOL_GROUP_DIM] = mixed.astype(BF16)
    pext_scr[0:MAX_WINDOW, :] = pext_scr[TOKEN_TILE:TOKEN_TILE + MAX_WINDOW, :]

    o_ref[0] = x + _dot(cat_scr[...], wout_ref[...])


def _mixer(x, g, w_in, cos, sin, dmask, qdec, kdec, gn, w_pool, pscale, w_out):
    b, s, _ = x.shape
    tile = pl.BlockSpec((1, TOKEN_TILE, D_MODEL), lambda bi, si: (bi, si, 0))
    rot = pl.BlockSpec((TOKEN_TILE, RET_HEAD_DIM), lambda bi, si: (si, 0))
    block_bytes = (4 * TOKEN_TILE * D_MODEL * 4 + D_MODEL * AB_IN_WIDTH * 2 + D_MODEL * D_MODEL * 2
                   + TOKEN_TILE * AB_IN_WIDTH * 4 + 4 * RET_TILE * RET_TILE * 4)
    return pl.pallas_call(
        _mixer_kernel,
        grid=(b, s // TOKEN_TILE),
        in_specs=[
            tile, _resident((1, D_MODEL)), _resident((D_MODEL, AB_IN_WIDTH)), rot, rot,
            _resident((RET_HEADS, RET_TILE, RET_TILE)),
            _resident((RET_HEADS, RET_TILE, RET_HEAD_DIM)),
            _resident((RET_HEADS, RET_TILE, RET_HEAD_DIM)),
            _resident((1, RET_WIDTH)),
            _resident((len(POOL_WINDOWS), POOL_GROUP_DIM, POOL_GROUP_DIM)),
            _resident((1, POOL_WIDTH)), _resident((D_MODEL, D_MODEL)),
        ],
        out_specs=tile,
        out_shape=jax.ShapeDtypeStruct(x.shape, F32),
        scratch_shapes=[
            pltpu.VMEM((TOKEN_TILE, AB_IN_WIDTH), F32),
            pltpu.VMEM((TOKEN_TILE, D_MODEL), BF16),
            pltpu.VMEM((RET_HEADS, RET_HEAD_DIM, RET_HEAD_DIM), F32),
            pltpu.VMEM((TOKEN_TILE + MAX_WINDOW, POOL_WIDTH), F32),
        ],
        compiler_params=_params(2, block_bytes),
        name="mixer_even",
    )(x, g, w_in, cos, sin, dmask, qdec, kdec, gn, w_pool, pscale, w_out)


def _qkv_kernel(x_ref, g_ref, w_ref, q_ref, k_ref, v_ref):
    h = _rms(x_ref[0], g_ref[...]).astype(BF16)
    for ref, col0, scale in ((q_ref, 0, ATT_HEAD_DIM ** -0.5 * LOG2_E), (k_ref, D_MODEL, None),
                             (v_ref, 2 * D_MODEL, None)):
        z = _dot(h, w_ref[:, col0:col0 + D_MODEL])
        if scale is not None:
            z = z * scale
        for pair in range(ATT_PAIRS):
            ref[0, pair] = z[:, pair * V7X_LANES:(pair + 1) * V7X_LANES].astype(BF16)


def _qkv(x, g, w):
    b, s, _ = x.shape
    tile = pl.BlockSpec((1, QKV_TOKEN_TILE, D_MODEL), lambda bi, si: (bi, si, 0))
    pair_tile = pl.BlockSpec((1, ATT_PAIRS, QKV_TOKEN_TILE, V7X_LANES), lambda bi, si: (bi, 0, si, 0))
    out = jax.ShapeDtypeStruct((b, ATT_PAIRS, s, V7X_LANES), BF16)
    block_bytes = (2 * QKV_TOKEN_TILE * D_MODEL * 4 + 6 * QKV_TOKEN_TILE * D_MODEL * 2
                   + 3 * D_MODEL * D_MODEL * 2)
    return pl.pallas_call(
        _qkv_kernel,
        grid=(b, s // QKV_TOKEN_TILE),
        in_specs=[tile, _resident((1, D_MODEL)), _resident((D_MODEL, 3 * D_MODEL))],
        out_specs=(pair_tile, pair_tile, pair_tile),
        out_shape=(out, out, out),
        compiler_params=_params(2, block_bytes),
        name="attn_qkv",
    )(x, g, w)


def _attn_kernel(x_ref, q_ref, kp_ref, kc_ref, vp_ref, vc_ref, bias_ref, wout_ref, o_ref,
                 kcat_scr, vcat_scr, opair_scr, *bufs):
    seq_tile = pl.program_id(1)

    kcat_scr[:, 0:TOKEN_TILE, :] = kp_ref[0]
    kcat_scr[:, TOKEN_TILE:2 * TOKEN_TILE, :] = kc_ref[0]

    lane = lax.broadcasted_iota(jnp.int32, (ATT_GROUP_ROWS, V7X_LANES), 1)
    first_head = lane < ATT_HEAD_DIM
    head_dim_row = lax.broadcasted_iota(jnp.int32, (V7X_LANES, ATT_GROUP_ROWS), 0) < ATT_HEAD_DIM
    key_row = lax.broadcasted_iota(jnp.int32, (ATT_GROUP_BAND, 2 * ATT_GROUP_ROWS), 0)
    ones_rows = jnp.ones((ATT_ONES_ROWS, ATT_GROUP_BAND), BF16)

    def row0(group):
        r0 = group * ATT_GROUP_ROWS
        return r0 if isinstance(group, int) else pl.multiple_of(r0, ATT_GROUP_ROWS)

    def scores(group, pair, s_ref, first_tile):
        r0 = row0(group)
        q = q_ref[0, pair, pl.ds(r0, ATT_GROUP_ROWS), :]
        zero = jnp.zeros_like(q)
        q2 = jnp.concatenate([jnp.where(first_head, q, zero), jnp.where(first_head, zero, q)], axis=0)
        s = _dot_nt(kcat_scr[pair, pl.ds(r0, ATT_GROUP_BAND), :], q2) + bias_ref[pair]
        if first_tile:
            s = s + jnp.where(key_row + r0 >= TOKEN_TILE, 0.0, NEG_INF)
        s_ref[...] = s

    def attend(group, pair, s_ref, p_ref):
        r0 = row0(group)
        m = jnp.max(s_ref[...], axis=0, keepdims=True)
        for rb in range(0, ATT_GROUP_BAND, ATT_SOFTMAX_ROWS):
            p_ref[rb:rb + ATT_SOFTMAX_ROWS, :] = jnp.exp2((s_ref[rb:rb + ATT_SOFTMAX_ROWS, :] - m).astype(BF16))
        v_t = vcat_scr[pair, pl.ds(r0, ATT_GROUP_BAND), :].T
        pv = _dot(jnp.concatenate([v_t, ones_rows], axis=0), p_ref[...])
        pv = pv[0:V7X_LANES, :] * (1.0 / pv[V7X_LANES:V7X_LANES + 1, :])
        out = jnp.where(head_dim_row, pv[:, 0:ATT_GROUP_ROWS], pv[:, ATT_GROUP_ROWS:2 * ATT_GROUP_ROWS])
        opair_scr[group, pair] = out.astype(BF16)

    def run(first_tile):
        score_scrs = bufs[:ATT_SCORE_BUFS]
        prob_scrs = bufs[ATT_SCORE_BUFS:]

        def block(group, last):
            for pair in range(ATT_PAIRS):
                ahead = pair + ATT_LOOKAHEAD
                if not (last and ahead >= ATT_PAIRS):
                    scores(group + ahead // ATT_PAIRS, ahead % ATT_PAIRS, score_scrs[ahead % ATT_SCORE_BUFS],
                           first_tile)
                attend(group, pair, score_scrs[pair % ATT_SCORE_BUFS], prob_scrs[pair % ATT_PROB_BUFS])

        def body(group, carry):
            block(group, False)
            return carry

        for pair in range(ATT_LOOKAHEAD):
            scores(0, pair, score_scrs[pair], first_tile)
        vcat_scr[:, 0:TOKEN_TILE, :] = vp_ref[0]
        vcat_scr[:, TOKEN_TILE:2 * TOKEN_TILE, :] = vc_ref[0]
        lax.fori_loop(0, ATT_GROUPS - 1, body, 0)
        block(ATT_GROUPS - 1, True)

    pl.when(seq_tile == 0)(functools.partial(run, True))
    pl.when(seq_tile != 0)(functools.partial(run, False))

    o_t = jnp.concatenate([opair_scr[group].reshape(D_MODEL, ATT_GROUP_ROWS) for group in range(ATT_GROUPS)],
                          axis=1)
    o_ref[0] = x_ref[0] + _dot_tn(o_t, wout_ref[...])


def _attn(x, q, k, v, bias, w_out):
    b, s, _ = x.shape
    tile = pl.BlockSpec((1, TOKEN_TILE, D_MODEL), lambda bi, si: (bi, si, 0))
    cur = pl.BlockSpec((1, ATT_PAIRS, TOKEN_TILE, V7X_LANES), lambda bi, si: (bi, 0, si, 0))
    prev = pl.BlockSpec((1, ATT_PAIRS, TOKEN_TILE, V7X_LANES),
                        lambda bi, si: (bi, 0, jnp.maximum(si - 1, 0), 0))
    score_buf = pltpu.VMEM((ATT_GROUP_BAND, 2 * ATT_GROUP_ROWS), F32)
    prob_buf = pltpu.VMEM((ATT_GROUP_BAND, 2 * ATT_GROUP_ROWS), BF16)
    block_bytes = (4 * TOKEN_TILE * D_MODEL * 4 + 10 * TOKEN_TILE * D_MODEL * 2 + D_MODEL * D_MODEL * 2
                   + ATT_HEADS * ATT_GROUP_ROWS * ATT_GROUP_BAND * 4 + 6 * TOKEN_TILE * D_MODEL * 2
                   + (4 * ATT_SCORE_BUFS + 2 * ATT_PROB_BUFS) * 2 * ATT_GROUP_ROWS * ATT_GROUP_BAND)
    return pl.pallas_call(
        _attn_kernel,
        grid=(b, s // TOKEN_TILE),
        in_specs=[
            tile, cur, prev, cur, prev, cur,
            _resident((ATT_PAIRS, ATT_GROUP_BAND, 2 * ATT_GROUP_ROWS)), _resident((D_MODEL, D_MODEL)),
        ],
        out_specs=tile,
        out_shape=jax.ShapeDtypeStruct(x.shape, F32),
        scratch_shapes=[
            pltpu.VMEM((ATT_PAIRS, 2 * TOKEN_TILE, V7X_LANES), BF16),
            pltpu.VMEM((ATT_PAIRS, 2 * TOKEN_TILE, V7X_LANES), BF16),
            pltpu.VMEM((ATT_GROUPS, ATT_PAIRS, V7X_LANES, ATT_GROUP_ROWS), BF16),
        ] + [score_buf] * ATT_SCORE_BUFS + [prob_buf] * ATT_PROB_BUFS,
        compiler_params=_params(2, block_bytes),
        name="attn_band",
    )(x, q, k, k, v, v, bias, w_out)


def kernel(x, mix_norm, ffn_norm, w_ffn_in, w_ffn_out, ab_w_in, ab_gn_gain, ab_w_pool, ab_pool_scale,
           ab_w_out, c_w_qkv, c_rel_bias, c_w_out, final_norm):
    b, s, d = x.shape
    depth = mix_norm.shape[0]
    assert d == D_MODEL and s % max(TOKEN_TILE, QKV_TOKEN_TILE, FFN_TOKEN_TILE) == 0, (x.shape,)
    n = b * s

    cos, sin = _rotary_tables(s)
    dmask, qdec, kdec = _decay_tables()
    fg = final_norm.reshape(1, D_MODEL)

    for layer in range(depth):
        i = layer // 2
        g_mix = mix_norm[layer].reshape(1, D_MODEL)
        if layer % 2 == 0:
            x = _mixer(x, g_mix, ab_w_in[i].astype(BF16), cos, sin, dmask, qdec, kdec,
                       ab_gn_gain[i].reshape(1, RET_WIDTH), ab_w_pool[i].astype(BF16),
                       ab_pool_scale[i].reshape(1, POOL_WIDTH), ab_w_out[i].astype(BF16))
        else:
            q, k, v = _qkv(x, g_mix, c_w_qkv[i].astype(BF16))
            bias = _bias_table(c_rel_bias[i]).reshape(ATT_PAIRS, 2 * ATT_GROUP_ROWS, ATT_GROUP_BAND)
            bias = jnp.transpose(bias, (0, 2, 1))
            x = _attn(x, q, k, v, bias, c_w_out[i].astype(BF16))
        x = _ffn(x.reshape(n, d), ffn_norm[layer].reshape(1, D_MODEL), w_ffn_in[layer].astype(BF16),
                 w_ffn_out[layer].astype(BF16), fg, final_norm=(layer == depth - 1)).reshape(b, s, d)
    return x
```

```python
import functools
import math

import jax
import jax.numpy as jnp
from jax import lax
from jax.experimental import pallas as pl
from jax.experimental.pallas import tpu as pltpu

F32 = jnp.float32
BF16 = jnp.bfloat16

D_MODEL = 1024
D_FF = 4 * D_MODEL
CHUNK = 64
RMS_EPS = 1e-6
RET_WIDTH = 512
RET_HEADS = 4
RET_HEAD_DIM = 128
RET_ROPE_BASE = 10000.0
GN_EPS = 1e-5
POOL_WIDTH = 512
POOL_WINDOWS = (2, 4, 8, 16)
POOL_GROUP_DIM = 128
MAX_WINDOW = max(POOL_WINDOWS)
AB_IN_WIDTH = 4 * RET_WIDTH + POOL_WIDTH
ATT_HEADS = 16
ATT_HEAD_DIM = 64
LEFT_CHUNKS = 8
BAND = (LEFT_CHUNKS + 1) * CHUNK
REL_CLIP = 128
N_REL = 2 * REL_CLIP + 1
NEG_INF = -1e30

V7X_LANES = 128
V7X_VMEM_BYTES = 64 * 1024 * 1024
V7X_VMEM_HEADROOM_BYTES = 8 * 1024 * 1024

TOKEN_TILE = 512
MIX_TOKEN_TILE = 1024
FFN_TOKEN_TILE = 1024
QKV_TOKEN_TILE = 1024
RET_TILE = 256
FFN_HIDDEN_TILE = 1024
ATT_GROUP_CHUNKS = 2
ATT_GROUP_ROWS = ATT_GROUP_CHUNKS * CHUNK
ATT_GROUP_BAND = BAND + (ATT_GROUP_CHUNKS - 1) * CHUNK
ATT_PAIRS = ATT_HEADS // 2
ATT_GROUPS = TOKEN_TILE // ATT_GROUP_ROWS
ATT_LOOKAHEAD = 3
ATT_SCORE_BUFS = 8
ATT_PROB_BUFS = 8
ATT_SOFTMAX_ROWS = 64
ATT_ONES_ROWS = 16
LOG2_E = math.log2(math.e)
REL_PAD = 3 * V7X_LANES
ROT_TABLE_TILE = 1024


def _vmem_limit(block_bytes):
    return int(min(block_bytes + 24 * 1024 * 1024, V7X_VMEM_BYTES - V7X_VMEM_HEADROOM_BYTES))


def _params(n_axes, block_bytes, flags=None):
    return pltpu.CompilerParams(
        dimension_semantics=("arbitrary",) * n_axes,
        vmem_limit_bytes=_vmem_limit(block_bytes),
        flags=flags,
    )


def _resident(shape):
    nd = len(shape)
    return pl.BlockSpec(shape, lambda *_: (0,) * nd, pipeline_mode=pl.Buffered(1))


def _rms(x, g):
    return x * lax.rsqrt(jnp.mean(x * x, axis=-1, keepdims=True) + RMS_EPS) * g


def _dot(a, b):
    return jnp.dot(a, b, preferred_element_type=F32)


def _dot_nt(a, b):
    return lax.dot_general(a, b, (((1,), (1,)), ((), ())), preferred_element_type=F32)


def _dot_tn(a, b):
    return lax.dot_general(a, b, (((0,), (0,)), ((), ())), preferred_element_type=F32)


def _rotary_table_kernel(cos_ref, sin_ref):
    rows = cos_ref.shape[0]
    n = (lax.broadcasted_iota(jnp.int32, (rows, RET_HEAD_DIM), 0) + pl.program_id(0) * rows).astype(F32)
    lane = lax.broadcasted_iota(jnp.int32, (rows, RET_HEAD_DIM), 1)
    frac = (lane >> 1).astype(F32) / float(RET_HEAD_DIM // 2 - 1)
    inv_freq = 1.0 / jnp.exp(frac * math.log(RET_ROPE_BASE))
    ang = n * inv_freq
    cos_ref[...] = jnp.cos(ang)
    sin_ref[...] = jnp.where((lane & 1) == 0, -1.0, 1.0) * jnp.sin(ang)


def _rotary_tables(seq):
    tile = min(ROT_TABLE_TILE, seq)
    shape = jax.ShapeDtypeStruct((seq, RET_HEAD_DIM), F32)
    spec = pl.BlockSpec((tile, RET_HEAD_DIM), lambda i: (i, 0))
    return pl.pallas_call(
        _rotary_table_kernel,
        grid=(seq // tile,),
        out_specs=(spec, spec),
        out_shape=(shape, shape),
        compiler_params=_params(1, 0),
        name="rotary_tables",
    )()


def _ret_log_gamma(h):
    return math.log1p(-(2.0 ** (-5.0 - h)))


def _decay_table_kernel(dmask_ref, qdec_ref, kdec_ref):
    t = lax.broadcasted_iota(jnp.int32, (RET_TILE, RET_TILE), 0)
    m = lax.broadcasted_iota(jnp.int32, (RET_TILE, RET_TILE), 1)
    dist = jnp.abs(t - m).astype(F32)
    chunk_shift = CHUNK.bit_length() - 1
    visible = (m >> chunk_shift) <= (t >> chunk_shift)
    tq = lax.broadcasted_iota(jnp.int32, (RET_TILE, RET_HEAD_DIM), 0).astype(F32)
    for h in range(RET_HEADS):
        lg = _ret_log_gamma(h)
        dmask_ref[h] = jnp.where(visible, jnp.exp(dist * lg), 0.0)
        qdec_ref[h] = jnp.exp((tq + 1.0) * lg)
        kdec_ref[h] = jnp.exp((RET_TILE - 1.0 - tq) * lg)


def _decay_tables():
    return pl.pallas_call(
        _decay_table_kernel,
        out_shape=(
            jax.ShapeDtypeStruct((RET_HEADS, RET_TILE, RET_TILE), F32),
            jax.ShapeDtypeStruct((RET_HEADS, RET_TILE, RET_HEAD_DIM), F32),
            jax.ShapeDtypeStruct((RET_HEADS, RET_TILE, RET_HEAD_DIM), F32),
        ),
        name="decay_tables",
    )()


_BIAS_ROW_TILE = 8
_BIAS_LEFT_PAD = CHUNK
_BIAS_EXT = 7 * V7X_LANES
_BIAS_WIDE = 6 * V7X_LANES


def _split3(x):
    hi = x.astype(BF16)
    r1 = x - hi.astype(F32)
    mid = r1.astype(BF16)
    lo = (r1 - mid.astype(F32)).astype(BF16)
    return hi, mid, lo


def _bias_table_kernel(rel_ref, out_ref):
    r = lax.broadcasted_iota(jnp.int32, (REL_PAD, _BIAS_EXT), 0)
    u = lax.broadcasted_iota(jnp.int32, (REL_PAD, _BIAS_EXT), 1)
    d = u - _BIAS_LEFT_PAD
    rel_idx = jnp.minimum(LEFT_CHUNKS * CHUNK + (CHUNK - 1) - d, REL_CLIP) + REL_CLIP
    inside = (d >= 0) & (d < BAND + CHUNK - 1)
    sel = jnp.where(inside & (r == rel_idx), 1.0, 0.0).astype(BF16)
    hi, mid, lo = _split3(rel_ref[...])
    diag = _dot(hi, sel) + _dot(mid, sel) + _dot(lo, sel)

    sub = lax.broadcasted_iota(jnp.int32, (_BIAS_ROW_TILE, _BIAS_WIDE), 0)
    col = lax.broadcasted_iota(jnp.int32, (_BIAS_ROW_TILE, ATT_GROUP_BAND), 1)
    for h in range(ATT_HEADS):
        row = diag[h:h + 1, :]
        skew = jnp.zeros((_BIAS_ROW_TILE, _BIAS_WIDE), F32)
        for b in range(_BIAS_ROW_TILE):
            start = _BIAS_ROW_TILE - 1 - b
            piece = jnp.broadcast_to(row[:, start:start + _BIAS_WIDE], (_BIAS_ROW_TILE, _BIAS_WIDE))
            skew = jnp.where(sub == b, piece, skew)
        for c in range(ATT_GROUP_CHUNKS):
            visible = (col >= c * CHUNK) & (col < c * CHUNK + BAND)
            for a in range(CHUNK // _BIAS_ROW_TILE):
                start = _BIAS_LEFT_PAD + (CHUNK - _BIAS_ROW_TILE) - c * CHUNK - _BIAS_ROW_TILE * a
                tile = skew[:, start:start + ATT_GROUP_BAND]
                r0 = c * CHUNK + _BIAS_ROW_TILE * a
                out_ref[h, r0:r0 + _BIAS_ROW_TILE, :] = jnp.where(visible, tile * LOG2_E, NEG_INF)


def _bias_table(rel_bias):
    rel = jnp.pad(rel_bias, ((0, 0), (0, REL_PAD - N_REL)))
    return pl.pallas_call(
        _bias_table_kernel,
        out_shape=jax.ShapeDtypeStruct((ATT_HEADS, ATT_GROUP_ROWS, ATT_GROUP_BAND), F32),
        name="bias_table",
    )(rel)


def _ffn_kernel(x_ref, g_ref, w1_ref, w2_ref, fg_ref, o_ref, *, final_norm):
    x = x_ref[...]
    h = _rms(x, g_ref[...]).astype(BF16)
    acc = x
    for j in range(D_FF // FFN_HIDDEN_TILE):
        cols = slice(j * FFN_HIDDEN_TILE, (j + 1) * FFN_HIDDEN_TILE)
        a = jnp.maximum(_dot(h, w1_ref[:, cols]), 0.0)
        acc = acc + _dot((a * a).astype(BF16), w2_ref[cols, :])
    if final_norm:
        acc = _rms(acc, fg_ref[...])
    o_ref[...] = acc


def _ffn(x2, g, w1, w2, fg, final_norm):
    n = x2.shape[0]
    tile = pl.BlockSpec((FFN_TOKEN_TILE, D_MODEL), lambda i: (i, 0))
    vec = _resident((1, D_MODEL))
    block_bytes = 4 * FFN_TOKEN_TILE * D_MODEL * 4 + 2 * D_MODEL * D_FF * 2
    return pl.pallas_call(
        functools.partial(_ffn_kernel, final_norm=final_norm),
        grid=(n // FFN_TOKEN_TILE,),
        in_specs=[tile, vec, _resident((D_MODEL, D_FF)), _resident((D_FF, D_MODEL)), vec],
        out_specs=tile,
        out_shape=jax.ShapeDtypeStruct((n, D_MODEL), F32),
        compiler_params=_params(1, block_bytes),
        name="ffn",
    )(x2, g, w1, w2, fg)


def _swap_pairs(x):
    lane = lax.broadcasted_iota(jnp.int32, x.shape, 1)
    return jnp.where((lane & 1) == 0, pltpu.roll(x, V7X_LANES - 1, 1), pltpu.roll(x, 1, 1))


def _mixer_kernel(x_ref, g_ref, win_ref, cos_ref, sin_ref, dmask_ref, qdec_ref, kdec_ref,
                  gn_ref, wpool_ref, pscale_ref, wout_ref, o_ref,
                  z_scr, cat_scr, state_scr, pext_scr):
    seq_tile = pl.program_id(1)

    @pl.when(seq_tile == 0)
    def _():
        state_scr[...] = jnp.zeros_like(state_scr)
        pext_scr[0:MAX_WINDOW, :] = jnp.zeros((MAX_WINDOW, POOL_WIDTH), F32)

    x = x_ref[0]
    h = _rms(x, g_ref[...]).astype(BF16)
    z_scr[...] = _dot(h, win_ref[...])

    for hd in range(RET_HEADS):
        lanes = slice(hd * RET_HEAD_DIM, (hd + 1) * RET_HEAD_DIM)
        state_decay = math.exp(RET_TILE * _ret_log_gamma(hd))
        for st in range(MIX_TOKEN_TILE // RET_TILE):
            rows = slice(st * RET_TILE, (st + 1) * RET_TILE)
            cos = cos_ref[rows, :]
            sin = sin_ref[rows, :]
            q = z_scr[rows, lanes]
            k = z_scr[rows, RET_WIDTH + hd * RET_HEAD_DIM:RET_WIDTH + (hd + 1) * RET_HEAD_DIM]
            v = z_scr[rows, 2 * RET_WIDTH + hd * RET_HEAD_DIM:2 * RET_WIDTH + (hd + 1) * RET_HEAD_DIM]
            gate = z_scr[rows, 3 * RET_WIDTH + hd * RET_HEAD_DIM:3 * RET_WIDTH + (hd + 1) * RET_HEAD_DIM]
            q = q * cos + _swap_pairs(q) * sin
            k = (k * cos + _swap_pairs(k) * sin) * (RET_HEAD_DIM ** -0.5)
            vb = v.astype(BF16)
            scores = _dot_nt(q.astype(BF16), k.astype(BF16)) * dmask_ref[hd]
            state = state_scr[hd]
            o = _dot(scores.astype(BF16), vb) + _dot((q * qdec_ref[hd]).astype(BF16), state.astype(BF16))
            state_scr[hd] = state * state_decay + _dot_tn((k * kdec_ref[hd]).astype(BF16), vb)
            mu = jnp.mean(o, axis=-1, keepdims=True)
            dev = o - mu
            var = jnp.mean(dev * dev, axis=-1, keepdims=True)
            y = dev * lax.rsqrt(var + GN_EPS) * gn_ref[:, lanes]
            cat_scr[rows, lanes] = (jax.nn.silu(gate) * y).astype(BF16)

    pext_scr[MAX_WINDOW:MAX_WINDOW + MIX_TOKEN_TILE, :] = z_scr[:, 4 * RET_WIDTH:4 * RET_WIDTH + POOL_WIDTH]
    pos_head = lax.broadcasted_iota(jnp.int32, (MAX_WINDOW, POOL_GROUP_DIM), 0) + seq_tile * MIX_TOKEN_TILE
    for gi, w in enumerate(POOL_WINDOWS):
        lanes = slice(gi * POOL_GROUP_DIM, (gi + 1) * POOL_GROUP_DIM)
        ext = pext_scr[:, lanes]
        total = ext
        span = 1
        while span < w:
            total = total + pltpu.roll(total, span, 0)
            span *= 2
        total = total[MAX_WINDOW:, :]
        tok = ext[MAX_WINDOW:, :]
        inv_count = jnp.concatenate(
            [1.0 / jnp.minimum(pos_head + 1, w).astype(F32),
             jnp.full((MIX_TOKEN_TILE - MAX_WINDOW, POOL_GROUP_DIM), 1.0 / w, F32)], axis=0)
        pooled = total * inv_count - tok
        mixed = _dot(pooled.astype(BF16), wpool_ref[gi]) * pscale_ref[:, lanes]
        cat_scr[:, RET_WIDTH + gi * POOL_GROUP_DIM:RET_WIDTH + (gi + 1) * POOL_GROUP_DIM] = mixed.astype(BF16)
    pext_scr[0:MAX_WINDOW, :] = pext_scr[MIX_TOKEN_TILE:MIX_TOKEN_TILE + MAX_WINDOW, :]

    o_ref[0] = x + _dot(cat_scr[...], wout_ref[...])


def _mixer(x, g, w_in, cos, sin, dmask, qdec, kdec, gn, w_pool, pscale, w_out):
    b, s, _ = x.shape
    tile = pl.BlockSpec((1, MIX_TOKEN_TILE, D_MODEL), lambda bi, si: (bi, si, 0))
    rot = pl.BlockSpec((MIX_TOKEN_TILE, RET_HEAD_DIM), lambda bi, si: (si, 0))
    block_bytes = (4 * MIX_TOKEN_TILE * D_MODEL * 4 + D_MODEL * AB_IN_WIDTH * 2 + D_MODEL * D_MODEL * 2
                   + MIX_TOKEN_TILE * AB_IN_WIDTH * 4 + 4 * RET_TILE * RET_TILE * 4)
    return pl.pallas_call(
        _mixer_kernel,
        grid=(b, s // MIX_TOKEN_TILE),
        in_specs=[
            tile, _resident((1, D_MODEL)), _resident((D_MODEL, AB_IN_WIDTH)), rot, rot,
            _resident((RET_HEADS, RET_TILE, RET_TILE)),
            _resident((RET_HEADS, RET_TILE, RET_HEAD_DIM)),
            _resident((RET_HEADS, RET_TILE, RET_HEAD_DIM)),
            _resident((1, RET_WIDTH)),
            _resident((len(POOL_WINDOWS), POOL_GROUP_DIM, POOL_GROUP_DIM)),
            _resident((1, POOL_WIDTH)), _resident((D_MODEL, D_MODEL)),
        ],
        out_specs=tile,
        out_shape=jax.ShapeDtypeStruct(x.shape, F32),
        scratch_shapes=[
            pltpu.VMEM((MIX_TOKEN_TILE, AB_IN_WIDTH), F32),
            pltpu.VMEM((MIX_TOKEN_TILE, D_MODEL), BF16),
            pltpu.VMEM((RET_HEADS, RET_HEAD_DIM, RET_HEAD_DIM), F32),
            pltpu.VMEM((MIX_TOKEN_TILE + MAX_WINDOW, POOL_WIDTH), F32),
        ],
        compiler_params=_params(2, block_bytes),
        name="mixer_even",
    )(x, g, w_in, cos, sin, dmask, qdec, kdec, gn, w_pool, pscale, w_out)


def _qkv_kernel(x_ref, g_ref, w_ref, q_ref, k_ref, v_ref):
    h = _rms(x_ref[0], g_ref[...]).astype(BF16)
    for ref, col0, scale in ((q_ref, 0, ATT_HEAD_DIM ** -0.5 * LOG2_E), (k_ref, D_MODEL, None),
                             (v_ref, 2 * D_MODEL, None)):
        z = _dot(h, w_ref[:, col0:col0 + D_MODEL])
        if scale is not None:
            z = z * scale
        for pair in range(ATT_PAIRS):
            ref[0, pair] = z[:, pair * V7X_LANES:(pair + 1) * V7X_LANES].astype(BF16)


def _qkv(x, g, w):
    b, s, _ = x.shape
    tile = pl.BlockSpec((1, QKV_TOKEN_TILE, D_MODEL), lambda bi, si: (bi, si, 0))
    pair_tile = pl.BlockSpec((1, ATT_PAIRS, QKV_TOKEN_TILE, V7X_LANES), lambda bi, si: (bi, 0, si, 0))
    out = jax.ShapeDtypeStruct((b, ATT_PAIRS, s, V7X_LANES), BF16)
    block_bytes = (2 * QKV_TOKEN_TILE * D_MODEL * 4 + 6 * QKV_TOKEN_TILE * D_MODEL * 2
                   + 3 * D_MODEL * D_MODEL * 2)
    return pl.pallas_call(
        _qkv_kernel,
        grid=(b, s // QKV_TOKEN_TILE),
        in_specs=[tile, _resident((1, D_MODEL)), _resident((D_MODEL, 3 * D_MODEL))],
        out_specs=(pair_tile, pair_tile, pair_tile),
        out_shape=(out, out, out),
        compiler_params=_params(2, block_bytes),
        name="attn_qkv",
    )(x, g, w)


def _attn_kernel(x_ref, q_ref, kp_ref, kc_ref, vp_ref, vc_ref, bias_ref, wout_ref, o_ref,
                 kcat_scr, vcat_scr, opair_scr, *bufs):
    seq_tile = pl.program_id(1)

    kcat_scr[:, 0:TOKEN_TILE, :] = kp_ref[0]
    kcat_scr[:, TOKEN_TILE:2 * TOKEN_TILE, :] = kc_ref[0]

    lane = lax.broadcasted_iota(jnp.int32, (ATT_GROUP_ROWS, V7X_LANES), 1)
    first_head = lane < ATT_HEAD_DIM
    head_dim_row = lax.broadcasted_iota(jnp.int32, (V7X_LANES, ATT_GROUP_ROWS), 0) < ATT_HEAD_DIM
    key_row = lax.broadcasted_iota(jnp.int32, (ATT_GROUP_BAND, 2 * ATT_GROUP_ROWS), 0)
    ones_rows = jnp.ones((ATT_ONES_ROWS, ATT_GROUP_BAND), BF16)

    def row0(group):
        r0 = group * ATT_GROUP_ROWS
        return r0 if isinstance(group, int) else pl.multiple_of(r0, ATT_GROUP_ROWS)

    def scores(group, pair, s_ref, first_tile):
        r0 = row0(group)
        q = q_ref[0, pair, pl.ds(r0, ATT_GROUP_ROWS), :]
        zero = jnp.zeros_like(q)
        q2 = jnp.concatenate([jnp.where(first_head, q, zero), jnp.where(first_head, zero, q)], axis=0)
        s = _dot_nt(kcat_scr[pair, pl.ds(r0, ATT_GROUP_BAND), :], q2) + bias_ref[pair]
        if first_tile:
            s = s + jnp.where(key_row + r0 >= TOKEN_TILE, 0.0, NEG_INF)
        s_ref[...] = s

    def attend(group, pair, s_ref, p_ref):
        r0 = row0(group)
        m = jnp.max(s_ref[...], axis=0, keepdims=True)
        for rb in range(0, ATT_GROUP_BAND, ATT_SOFTMAX_ROWS):
            p_ref[rb:rb + ATT_SOFTMAX_ROWS, :] = jnp.exp2((s_ref[rb:rb + ATT_SOFTMAX_ROWS, :] - m).astype(BF16))
        v_t = vcat_scr[pair, pl.ds(r0, ATT_GROUP_BAND), :].T
        pv = _dot(jnp.concatenate([v_t, ones_rows], axis=0), p_ref[...])
        pv = pv[0:V7X_LANES, :] * (1.0 / pv[V7X_LANES:V7X_LANES + 1, :])
        out = jnp.where(head_dim_row, pv[:, 0:ATT_GROUP_ROWS], pv[:, ATT_GROUP_ROWS:2 * ATT_GROUP_ROWS])
        opair_scr[group, pair] = out.astype(BF16)

    def run(first_tile):
        score_scrs = bufs[:ATT_SCORE_BUFS]
        prob_scrs = bufs[ATT_SCORE_BUFS:]

        def block(group, last):
            for pair in range(ATT_PAIRS):
                ahead = pair + ATT_LOOKAHEAD
                if not (last and ahead >= ATT_PAIRS):
                    scores(group + ahead // ATT_PAIRS, ahead % ATT_PAIRS, score_scrs[ahead % ATT_SCORE_BUFS],
                           first_tile)
                attend(group, pair, score_scrs[pair % ATT_SCORE_BUFS], prob_scrs[pair % ATT_PROB_BUFS])

        def body(group, carry):
            block(group, False)
            return carry

        for pair in range(ATT_LOOKAHEAD):
            scores(0, pair, score_scrs[pair], first_tile)
        vcat_scr[:, 0:TOKEN_TILE, :] = vp_ref[0]
        vcat_scr[:, TOKEN_TILE:2 * TOKEN_TILE, :] = vc_ref[0]
        lax.fori_loop(0, ATT_GROUPS - 1, body, 0)
        block(ATT_GROUPS - 1, True)

    pl.when(seq_tile == 0)(functools.partial(run, True))
    pl.when(seq_tile != 0)(functools.partial(run, False))

    o_t = jnp.concatenate([opair_scr[group].reshape(D_MODEL, ATT_GROUP_ROWS) for group in range(ATT_GROUPS)],
                          axis=1)
    o_ref[0] = x_ref[0] + _dot_tn(o_t, wout_ref[...])


def _attn(x, q, k, v, bias, w_out):
    b, s, _ = x.shape
    tile = pl.BlockSpec((1, TOKEN_TILE, D_MODEL), lambda bi, si: (bi, si, 0))
    cur = pl.BlockSpec((1, ATT_PAIRS, TOKEN_TILE, V7X_LANES), lambda bi, si: (bi, 0, si, 0))
    prev = pl.BlockSpec((1, ATT_PAIRS, TOKEN_TILE, V7X_LANES),
                        lambda bi, si: (bi, 0, jnp.maximum(si - 1, 0), 0))
    score_buf = pltpu.VMEM((ATT_GROUP_BAND, 2 * ATT_GROUP_ROWS), F32)
    prob_buf = pltpu.VMEM((ATT_GROUP_BAND, 2 * ATT_GROUP_ROWS), BF16)
    block_bytes = (4 * TOKEN_TILE * D_MODEL * 4 + 10 * TOKEN_TILE * D_MODEL * 2 + D_MODEL * D_MODEL * 2
                   + ATT_HEADS * ATT_GROUP_ROWS * ATT_GROUP_BAND * 4 + 6 * TOKEN_TILE * D_MODEL * 2
                   + (4 * ATT_SCORE_BUFS + 2 * ATT_PROB_BUFS) * 2 * ATT_GROUP_ROWS * ATT_GROUP_BAND)
    return pl.pallas_call(
        _attn_kernel,
        grid=(b, s // TOKEN_TILE),
        in_specs=[
            tile, cur, prev, cur, prev, cur,
            _resident((ATT_PAIRS, ATT_GROUP_BAND, 2 * ATT_GROUP_ROWS)), _resident((D_MODEL, D_MODEL)),
        ],
        out_specs=tile,
        out_shape=jax.ShapeDtypeStruct(x.shape, F32),
        scratch_shapes=[
            pltpu.VMEM((ATT_PAIRS, 2 * TOKEN_TILE, V7X_LANES), BF16),
            pltpu.VMEM((ATT_PAIRS, 2 * TOKEN_TILE, V7X_LANES), BF16),
            pltpu.VMEM((ATT_GROUPS, ATT_PAIRS, V7X_LANES, ATT_GROUP_ROWS), BF16),
        ] + [score_buf] * ATT_SCORE_BUFS + [prob_buf] * ATT_PROB_BUFS,
        compiler_params=_params(2, block_bytes),
        name="attn_band",
    )(x, q, k, k, v, v, bias, w_out)


def kernel(x, mix_norm, ffn_norm, w_ffn_in, w_ffn_out, ab_w_in, ab_gn_gain, ab_w_pool, ab_pool_scale,
           ab_w_out, c_w_qkv, c_rel_bias, c_w_out, final_norm):
    b, s, d = x.shape
    depth = mix_norm.shape[0]
    assert d == D_MODEL and s % max(TOKEN_TILE, MIX_TOKEN_TILE, QKV_TOKEN_TILE, FFN_TOKEN_TILE) == 0, (x.shape,)
    n = b * s

    cos, sin = _rotary_tables(s)
    dmask, qdec, kdec = _decay_tables()
    fg = final_norm.reshape(1, D_MODEL)

    for layer in range(depth):
        i = layer // 2
        g_mix = mix_norm[layer].reshape(1, D_MODEL)
        if layer % 2 == 0:
            x = _mixer(x, g_mix, ab_w_in[i].astype(BF16), cos, sin, dmask, qdec, kdec,
                       ab_gn_gain[i].reshape(1, RET_WIDTH), ab_w_pool[i].astype(BF16),
                       ab_pool_scale[i].reshape(1, POOL_WIDTH), ab_w_out[i].astype(BF16))
        else:
            q, k, v = _qkv(x, g_mix, c_w_qkv[i].astype(BF16))
            bias = _bias_table(c_rel_bias[i]).reshape(ATT_PAIRS, 2 * ATT_GROUP_ROWS, ATT_GROUP_BAND)
            bias = jnp.transpose(bias, (0, 2, 1))
            x = _attn(x, q, k, v, bias, c_w_out[i].astype(BF16))
        x = _ffn(x.reshape(n, d), ffn_norm[layer].reshape(1, D_MODEL), w_ffn_in[layer].astype(BF16),
                 w_ffn_out[layer].astype(BF16), fg, final_norm=(layer == depth - 1)).reshape(b, s, d)
    return x
```

```python
import functools
import math

import jax
import jax.numpy as jnp
from jax import lax
from jax.experimental import pallas as pl
from jax.experimental.pallas import tpu as pltpu

F32 = jnp.float32
BF16 = jnp.bfloat16

D_MODEL = 1024
D_FF = 4 * D_MODEL
CHUNK = 64
RMS_EPS = 1e-6
RET_WIDTH = 512
RET_HEADS = 4
RET_HEAD_DIM = 128
RET_ROPE_BASE = 10000.0
GN_EPS = 1e-5
POOL_WIDTH = 512
POOL_WINDOWS = (2, 4, 8, 16)
POOL_GROUP_DIM = 128
MAX_WINDOW = max(POOL_WINDOWS)
AB_IN_WIDTH = 4 * RET_WIDTH + POOL_WIDTH
ATT_HEADS = 16
ATT_HEAD_DIM = 64
LEFT_CHUNKS = 8
BAND = (LEFT_CHUNKS + 1) * CHUNK
REL_CLIP = 128
N_REL = 2 * REL_CLIP + 1
NEG_INF = -1e30

V7X_LANES = 128
V7X_VMEM_BYTES = 64 * 1024 * 1024
V7X_VMEM_HEADROOM_BYTES = 8 * 1024 * 1024

TOKEN_TILE = 512
MIX_TOKEN_TILE = 1024
FFN_TOKEN_TILE = 1024
QKV_TOKEN_TILE = 1024
RET_TILE = 256
FFN_HIDDEN_TILE = 1024
ATT_GROUP_CHUNKS = 2
ATT_GROUP_ROWS = ATT_GROUP_CHUNKS * CHUNK
ATT_GROUP_BAND = BAND + (ATT_GROUP_CHUNKS - 1) * CHUNK
ATT_PAIRS = ATT_HEADS // 2
ATT_GROUPS = TOKEN_TILE // ATT_GROUP_ROWS
ATT_LOOKAHEAD = 3
ATT_SCORE_BUFS = 8
ATT_PROB_BUFS = 8
ATT_SOFTMAX_ROWS = 64
ATT_ONES_ROWS = 16
LOG2_E = math.log2(math.e)
REL_PAD = 3 * V7X_LANES
ROT_TABLE_TILE = 1024


def _vmem_limit(block_bytes):
    return int(min(block_bytes + 24 * 1024 * 1024, V7X_VMEM_BYTES - V7X_VMEM_HEADROOM_BYTES))


def _params(n_axes, block_bytes, flags=None):
    return pltpu.CompilerParams(
        dimension_semantics=("arbitrary",) * n_axes,
        vmem_limit_bytes=_vmem_limit(block_bytes),
        flags=flags,
    )


def _resident(shape):
    nd = len(shape)
    return pl.BlockSpec(shape, lambda *_: (0,) * nd, pipeline_mode=pl.Buffered(1))


def _rms(x, g):
    return x * lax.rsqrt(jnp.mean(x * x, axis=-1, keepdims=True) + RMS_EPS) * g


def _dot(a, b):
    return jnp.dot(a, b, preferred_element_type=F32)


def _dot_nt(a, b):
    return lax.dot_general(a, b, (((1,), (1,)), ((), ())), preferred_element_type=F32)


def _dot_tn(a, b):
    return lax.dot_general(a, b, (((0,), (0,)), ((), ())), preferred_element_type=F32)


def _rotary_table_kernel(cos_ref, sin_ref):
    rows = cos_ref.shape[0]
    n = (lax.broadcasted_iota(jnp.int32, (rows, RET_HEAD_DIM), 0) + pl.program_id(0) * rows).astype(F32)
    lane = lax.broadcasted_iota(jnp.int32, (rows, RET_HEAD_DIM), 1)
    frac = (lane >> 1).astype(F32) / float(RET_HEAD_DIM // 2 - 1)
    inv_freq = 1.0 / jnp.exp(frac * math.log(RET_ROPE_BASE))
    ang = n * inv_freq
    cos_ref[...] = jnp.cos(ang)
    sin_ref[...] = jnp.where((lane & 1) == 0, -1.0, 1.0) * jnp.sin(ang)


def _rotary_tables(seq):
    tile = min(ROT_TABLE_TILE, seq)
    shape = jax.ShapeDtypeStruct((seq, RET_HEAD_DIM), F32)
    spec = pl.BlockSpec((tile, RET_HEAD_DIM), lambda i: (i, 0))
    return pl.pallas_call(
        _rotary_table_kernel,
        grid=(seq // tile,),
        out_specs=(spec, spec),
        out_shape=(shape, shape),
        compiler_params=_params(1, 0),
        name="rotary_tables",
    )()


def _ret_log_gamma(h):
    return math.log1p(-(2.0 ** (-5.0 - h)))


def _decay_table_kernel(dmask_ref, qdec_ref, kdec_ref):
    t = lax.broadcasted_iota(jnp.int32, (RET_TILE, RET_TILE), 0)
    m = lax.broadcasted_iota(jnp.int32, (RET_TILE, RET_TILE), 1)
    dist = jnp.abs(t - m).astype(F32)
    chunk_shift = CHUNK.bit_length() - 1
    visible = (m >> chunk_shift) <= (t >> chunk_shift)
    tq = lax.broadcasted_iota(jnp.int32, (RET_TILE, RET_HEAD_DIM), 0).astype(F32)
    for h in range(RET_HEADS):
        lg = _ret_log_gamma(h)
        dmask_ref[h] = jnp.where(visible, jnp.exp(dist * lg), 0.0)
        qdec_ref[h] = jnp.exp((tq + 1.0) * lg)
        kdec_ref[h] = jnp.exp((RET_TILE - 1.0 - tq) * lg)


def _decay_tables():
    return pl.pallas_call(
        _decay_table_kernel,
        out_shape=(
            jax.ShapeDtypeStruct((RET_HEADS, RET_TILE, RET_TILE), F32),
            jax.ShapeDtypeStruct((RET_HEADS, RET_TILE, RET_HEAD_DIM), F32),
            jax.ShapeDtypeStruct((RET_HEADS, RET_TILE, RET_HEAD_DIM), F32),
        ),
        name="decay_tables",
    )()


_BIAS_ROW_TILE = 8
_BIAS_LEFT_PAD = CHUNK
_BIAS_EXT = 7 * V7X_LANES
_BIAS_WIDE = 6 * V7X_LANES


def _split3(x):
    hi = x.astype(BF16)
    r1 = x - hi.astype(F32)
    mid = r1.astype(BF16)
    lo = (r1 - mid.astype(F32)).astype(BF16)
    return hi, mid, lo


def _bias_table_kernel(rel_ref, out_ref):
    r = lax.broadcasted_iota(jnp.int32, (REL_PAD, _BIAS_EXT), 0)
    u = lax.broadcasted_iota(jnp.int32, (REL_PAD, _BIAS_EXT), 1)
    d = u - _BIAS_LEFT_PAD
    rel_idx = jnp.minimum(LEFT_CHUNKS * CHUNK + (CHUNK - 1) - d, REL_CLIP) + REL_CLIP
    inside = (d >= 0) & (d < BAND + CHUNK - 1)
    sel = jnp.where(inside & (r == rel_idx), 1.0, 0.0).astype(BF16)
    hi, mid, lo = _split3(rel_ref[...])
    diag = _dot(hi, sel) + _dot(mid, sel) + _dot(lo, sel)

    sub = lax.broadcasted_iota(jnp.int32, (_BIAS_ROW_TILE, _BIAS_WIDE), 0)
    col = lax.broadcasted_iota(jnp.int32, (_BIAS_ROW_TILE, ATT_GROUP_BAND), 1)
    for h in range(ATT_HEADS):
        row = diag[h:h + 1, :]
        skew = jnp.zeros((_BIAS_ROW_TILE, _BIAS_WIDE), F32)
        for b in range(_BIAS_ROW_TILE):
            start = _BIAS_ROW_TILE - 1 - b
            piece = jnp.broadcast_to(row[:, start:start + _BIAS_WIDE], (_BIAS_ROW_TILE, _BIAS_WIDE))
            skew = jnp.where(sub == b, piece, skew)
        for c in range(ATT_GROUP_CHUNKS):
            visible = (col >= c * CHUNK) & (col < c * CHUNK + BAND)
            for a in range(CHUNK // _BIAS_ROW_TILE):
                start = _BIAS_LEFT_PAD + (CHUNK - _BIAS_ROW_TILE) - c * CHUNK - _BIAS_ROW_TILE * a
                tile = skew[:, start:start + ATT_GROUP_BAND]
                r0 = c * CHUNK + _BIAS_ROW_TILE * a
                out_ref[h, r0:r0 + _BIAS_ROW_TILE, :] = jnp.where(visible, tile * LOG2_E, NEG_INF)


def _bias_table(rel_bias):
    rel = jnp.pad(rel_bias, ((0, 0), (0, REL_PAD - N_REL)))
    return pl.pallas_call(
        _bias_table_kernel,
        out_shape=jax.ShapeDtypeStruct((ATT_HEADS, ATT_GROUP_ROWS, ATT_GROUP_BAND), F32),
        name="bias_table",
    )(rel)


def _ffn_kernel(x_ref, g_ref, w1_ref, w2_ref, fg_ref, o_ref, *, final_norm):
    x = x_ref[...]
    h = _rms(x, g_ref[...]).astype(BF16)
    acc = x
    for j in range(D_FF // FFN_HIDDEN_TILE):
        cols = slice(j * FFN_HIDDEN_TILE, (j + 1) * FFN_HIDDEN_TILE)
        a = jnp.maximum(_dot(h, w1_ref[:, cols]), 0.0)
        acc = acc + _dot((a * a).astype(BF16), w2_ref[cols, :])
    if final_norm:
        acc = _rms(acc, fg_ref[...])
    o_ref[...] = acc


def _ffn(x2, g, w1, w2, fg, final_norm):
    n = x2.shape[0]
    tile = pl.BlockSpec((FFN_TOKEN_TILE, D_MODEL), lambda i: (i, 0))
    vec = _resident((1, D_MODEL))
    block_bytes = 4 * FFN_TOKEN_TILE * D_MODEL * 4 + 2 * D_MODEL * D_FF * 2
    return pl.pallas_call(
        functools.partial(_ffn_kernel, final_norm=final_norm),
        grid=(n // FFN_TOKEN_TILE,),
        in_specs=[tile, vec, _resident((D_MODEL, D_FF)), _resident((D_FF, D_MODEL)), vec],
        out_specs=tile,
        out_shape=jax.ShapeDtypeStruct((n, D_MODEL), F32),
        compiler_params=_params(1, block_bytes),
        name="ffn",
    )(x2, g, w1, w2, fg)


def _swap_pairs(x):
    lane = lax.broadcasted_iota(jnp.int32, x.shape, 1)
    return jnp.where((lane & 1) == 0, pltpu.roll(x, V7X_LANES - 1, 1), pltpu.roll(x, 1, 1))


def _mixer_kernel(x_ref, g_ref, win_ref, cos_ref, sin_ref, dmask_ref, qdec_ref, kdec_ref,
                  gn_ref, wpool_ref, pscale_ref, wout_ref, o_ref,
                  z_scr, cat_scr, state_scr, pext_scr):
    seq_tile = pl.program_id(1)

    @pl.when(seq_tile == 0)
    def _():
        state_scr[...] = jnp.zeros_like(state_scr)
        pext_scr[0:MAX_WINDOW, :] = jnp.zeros((MAX_WINDOW, POOL_WIDTH), F32)

    x = x_ref[0]
    h = _rms(x, g_ref[...]).astype(BF16)
    z_scr[...] = _dot(h, win_ref[...])

    for hd in range(RET_HEADS):
        lanes = slice(hd * RET_HEAD_DIM, (hd + 1) * RET_HEAD_DIM)
        state_decay = math.exp(RET_TILE * _ret_log_gamma(hd))
        for st in range(MIX_TOKEN_TILE // RET_TILE):
            rows = slice(st * RET_TILE, (st + 1) * RET_TILE)
            cos = cos_ref[rows, :]
            sin = sin_ref[rows, :]
            q = z_scr[rows, lanes]
            k = z_scr[rows, RET_WIDTH + hd * RET_HEAD_DIM:RET_WIDTH + (hd + 1) * RET_HEAD_DIM]
            v = z_scr[rows, 2 * RET_WIDTH + hd * RET_HEAD_DIM:2 * RET_WIDTH + (hd + 1) * RET_HEAD_DIM]
            gate = z_scr[rows, 3 * RET_WIDTH + hd * RET_HEAD_DIM:3 * RET_WIDTH + (hd + 1) * RET_HEAD_DIM]
            q = q * cos + _swap_pairs(q) * sin
            k = (k * cos + _swap_pairs(k) * sin) * (RET_HEAD_DIM ** -0.5)
            vb = v.astype(BF16)
            scores = _dot_nt(q.astype(BF16), k.astype(BF16)) * dmask_ref[hd]
            state = state_scr[hd]
            o = _dot(scores.astype(BF16), vb) + _dot((q * qdec_ref[hd]).astype(BF16), state.astype(BF16))
            state_scr[hd] = state * state_decay + _dot_tn((k * kdec_ref[hd]).astype(BF16), vb)
            mu = jnp.mean(o, axis=-1, keepdims=True)
            dev = o - mu
            var = jnp.mean(dev * dev, axis=-1, keepdims=True)
            y = dev * lax.rsqrt(var + GN_EPS) * gn_ref[:, lanes]
            cat_scr[rows, lanes] = (jax.nn.silu(gate) * y).astype(BF16)

    pext_scr[MAX_WINDOW:MAX_WINDOW + MIX_TOKEN_TILE, :] = z_scr[:, 4 * RET_WIDTH:4 * RET_WIDTH + POOL_WIDTH]
    pos_head = lax.broadcasted_iota(jnp.int32, (MAX_WINDOW, POOL_GROUP_DIM), 0) + seq_tile * MIX_TOKEN_TILE
    for gi, w in enumerate(POOL_WINDOWS):
        lanes = slice(gi * POOL_GROUP_DIM, (gi + 1) * POOL_GROUP_DIM)
        ext = pext_scr[:, lanes]
        total = ext
        span = 1
        while span < w:
            total = total + pltpu.roll(total, span, 0)
            span *= 2
        total = total[MAX_WINDOW:, :]
        tok = ext[MAX_WINDOW:, :]
        inv_count = jnp.concatenate(
            [1.0 / jnp.minimum(pos_head + 1, w).astype(F32),
             jnp.full((MIX_TOKEN_TILE - MAX_WINDOW, POOL_GROUP_DIM), 1.0 / w, F32)], axis=0)
        pooled = total * inv_count - tok
        mixed = _dot(pooled.astype(BF16), wpool_ref[gi]) * pscale_ref[:, lanes]
        cat_scr[:, RET_WIDTH + gi * POOL_GROUP_DIM:RET_WIDTH + (gi + 1) * POOL_GROUP_DIM] = mixed.astype(BF16)
    pext_scr[0:MAX_WINDOW, :] = pext_scr[MIX_TOKEN_TILE:MIX_TOKEN_TILE + MAX_WINDOW, :]

    o_ref[0] = x + _dot(cat_scr[...], wout_ref[...])


def _mixer(x, g, w_in, cos, sin, dmask, qdec, kdec, gn, w_pool, pscale, w_out):
    b, s, _ = x.shape
    tile = pl.BlockSpec((1, MIX_TOKEN_TILE, D_MODEL), lambda bi, si: (bi, si, 0))
    rot = pl.BlockSpec((MIX_TOKEN_TILE, RET_HEAD_DIM), lambda bi, si: (si, 0))
    block_bytes = (4 * MIX_TOKEN_TILE * D_MODEL * 4 + D_MODEL * AB_IN_WIDTH * 2 + D_MODEL * D_MODEL * 2
                   + MIX_TOKEN_TILE * AB_IN_WIDTH * 4 + 4 * RET_TILE * RET_TILE * 4)
    return pl.pallas_call(
        _mixer_kernel,
        grid=(b, s // MIX_TOKEN_TILE),
        in_specs=[
            tile, _resident((1, D_MODEL)), _resident((D_MODEL, AB_IN_WIDTH)), rot, rot,
            _resident((RET_HEADS, RET_TILE, RET_TILE)),
            _resident((RET_HEADS, RET_TILE, RET_HEAD_DIM)),
            _resident((RET_HEADS, RET_TILE, RET_HEAD_DIM)),
            _resident((1, RET_WIDTH)),
            _resident((len(POOL_WINDOWS), POOL_GROUP_DIM, POOL_GROUP_DIM)),
            _resident((1, POOL_WIDTH)), _resident((D_MODEL, D_MODEL)),
        ],
        out_specs=tile,
        out_shape=jax.ShapeDtypeStruct(x.shape, F32),
        scratch_shapes=[
            pltpu.VMEM((MIX_TOKEN_TILE, AB_IN_WIDTH), F32),
            pltpu.VMEM((MIX_TOKEN_TILE, D_MODEL), BF16),
            pltpu.VMEM((RET_HEADS, RET_HEAD_DIM, RET_HEAD_DIM), F32),
            pltpu.VMEM((MIX_TOKEN_TILE + MAX_WINDOW, POOL_WIDTH), F32),
        ],
        compiler_params=_params(2, block_bytes),
        name="mixer_even",
    )(x, g, w_in, cos, sin, dmask, qdec, kdec, gn, w_pool, pscale, w_out)


def _qkv_kernel(x_ref, g_ref, w_ref, q_ref, k_ref, v_ref):
    h = _rms(x_ref[0], g_ref[...]).astype(BF16)
    for ref, col0, scale in ((q_ref, 0, ATT_HEAD_DIM ** -0.5 * LOG2_E), (k_ref, D_MODEL, None),
                             (v_ref, 2 * D_MODEL, None)):
        z = _dot(h, w_ref[:, col0:col0 + D_MODEL])
        if scale is not None:
            z = z * scale
        for pair in range(ATT_PAIRS):
            ref[0, pair] = z[:, pair * V7X_LANES:(pair + 1) * V7X_LANES].astype(BF16)


def _qkv(x, g, w):
    b, s, _ = x.shape
    tile = pl.BlockSpec((1, QKV_TOKEN_TILE, D_MODEL), lambda bi, si: (bi, si, 0))
    pair_tile = pl.BlockSpec((1, ATT_PAIRS, QKV_TOKEN_TILE, V7X_LANES), lambda bi, si: (bi, 0, si, 0))
    out = jax.ShapeDtypeStruct((b, ATT_PAIRS, s, V7X_LANES), BF16)
    block_bytes = (2 * QKV_TOKEN_TILE * D_MODEL * 4 + 6 * QKV_TOKEN_TILE * D_MODEL * 2
                   + 3 * D_MODEL * D_MODEL * 2)
    return pl.pallas_call(
        _qkv_kernel,
        grid=(b, s // QKV_TOKEN_TILE),
        in_specs=[tile, _resident((1, D_MODEL)), _resident((D_MODEL, 3 * D_MODEL))],
        out_specs=(pair_tile, pair_tile, pair_tile),
        out_shape=(out, out, out),
        compiler_params=_params(2, block_bytes),
        name="attn_qkv",
    )(x, g, w)


def _attn_kernel(x_ref, q_ref, kp_ref, kc_ref, vp_ref, vc_ref, bias_ref, wout_ref, o_ref,
                 kcat_scr, vcat_scr, opair_scr, *bufs):
    seq_tile = pl.program_id(1)

    kcat_scr[:, 0:TOKEN_TILE, :] = kp_ref[0]
    kcat_scr[:, TOKEN_TILE:TOKEN_TILE + ATT_GROUP_ROWS, :] = kc_ref[0, :, 0:ATT_GROUP_ROWS, :]

    lane = lax.broadcasted_iota(jnp.int32, (ATT_GROUP_ROWS, V7X_LANES), 1)
    first_head = lane < ATT_HEAD_DIM
    head_dim_row = lax.broadcasted_iota(jnp.int32, (V7X_LANES, ATT_GROUP_ROWS), 0) < ATT_HEAD_DIM
    key_row = lax.broadcasted_iota(jnp.int32, (ATT_GROUP_BAND, 2 * ATT_GROUP_ROWS), 0)
    ones_rows = jnp.ones((ATT_ONES_ROWS, ATT_GROUP_BAND), BF16)

    def row0(group):
        r0 = group * ATT_GROUP_ROWS
        return r0 if isinstance(group, int) else pl.multiple_of(r0, ATT_GROUP_ROWS)

    def scores(group, pair, s_ref, first_tile):
        r0 = row0(group)
        q = q_ref[0, pair, pl.ds(r0, ATT_GROUP_ROWS), :]
        zero = jnp.zeros_like(q)
        q2 = jnp.concatenate([jnp.where(first_head, q, zero), jnp.where(first_head, zero, q)], axis=0)
        s = _dot_nt(kcat_scr[pair, pl.ds(r0, ATT_GROUP_BAND), :], q2) + bias_ref[pair]
        if first_tile:
            s = s + jnp.where(key_row + r0 >= TOKEN_TILE, 0.0, NEG_INF)
        s_ref[...] = s

    def attend(group, pair, s_ref, p_ref):
        r0 = row0(group)
        m = jnp.max(s_ref[...], axis=0, keepdims=True)
        for rb in range(0, ATT_GROUP_BAND, ATT_SOFTMAX_ROWS):
            p_ref[rb:rb + ATT_SOFTMAX_ROWS, :] = jnp.exp2((s_ref[rb:rb + ATT_SOFTMAX_ROWS, :] - m).astype(BF16))
        v_t = vcat_scr[pair, pl.ds(r0, ATT_GROUP_BAND), :].T
        pv = _dot(jnp.concatenate([v_t, ones_rows], axis=0), p_ref[...])
        pv = pv[0:V7X_LANES, :] * (1.0 / pv[V7X_LANES:V7X_LANES + 1, :])
        out = jnp.where(head_dim_row, pv[:, 0:ATT_GROUP_ROWS], pv[:, ATT_GROUP_ROWS:2 * ATT_GROUP_ROWS])
        opair_scr[group, pair] = out.astype(BF16)

    def run(first_tile):
        score_scrs = bufs[:ATT_SCORE_BUFS]
        prob_scrs = bufs[ATT_SCORE_BUFS:]

        def block(group, last):
            for pair in range(ATT_PAIRS):
                ahead = pair + ATT_LOOKAHEAD
                if not (last and ahead >= ATT_PAIRS):
                    scores(group + ahead // ATT_PAIRS, ahead % ATT_PAIRS, score_scrs[ahead % ATT_SCORE_BUFS],
                           first_tile)
                attend(group, pair, score_scrs[pair % ATT_SCORE_BUFS], prob_scrs[pair % ATT_PROB_BUFS])

        def body(group, carry):
            block(group, False)
            return carry

        for pair in range(ATT_LOOKAHEAD):
            scores(0, pair, score_scrs[pair], first_tile)
        vcat_scr[:, 0:TOKEN_TILE, :] = vp_ref[0]
        vcat_scr[:, TOKEN_TILE:2 * TOKEN_TILE, :] = vc_ref[0]
        kcat_scr[:, TOKEN_TILE + ATT_GROUP_ROWS:2 * TOKEN_TILE, :] = kc_ref[0, :, ATT_GROUP_ROWS:TOKEN_TILE, :]
        lax.fori_loop(0, ATT_GROUPS - 1, body, 0)
        block(ATT_GROUPS - 1, True)

    pl.when(seq_tile == 0)(functools.partial(run, True))
    pl.when(seq_tile != 0)(functools.partial(run, False))

    o_t = jnp.concatenate([opair_scr[group].reshape(D_MODEL, ATT_GROUP_ROWS) for group in range(ATT_GROUPS)],
                          axis=1)
    o_ref[0] = x_ref[0] + _dot_tn(o_t, wout_ref[...])


def _attn(x, q, k, v, bias, w_out):
    b, s, _ = x.shape
    tile = pl.BlockSpec((1, TOKEN_TILE, D_MODEL), lambda bi, si: (bi, si, 0))
    cur = pl.BlockSpec((1, ATT_PAIRS, TOKEN_TILE, V7X_LANES), lambda bi, si: (bi, 0, si, 0))
    prev = pl.BlockSpec((1, ATT_PAIRS, TOKEN_TILE, V7X_LANES),
                        lambda bi, si: (bi, 0, jnp.maximum(si - 1, 0), 0))
    score_buf = pltpu.VMEM((ATT_GROUP_BAND, 2 * ATT_GROUP_ROWS), F32)
    prob_buf = pltpu.VMEM((ATT_GROUP_BAND, 2 * ATT_GROUP_ROWS), BF16)
    block_bytes = (4 * TOKEN_TILE * D_MODEL * 4 + 10 * TOKEN_TILE * D_MODEL * 2 + D_MODEL * D_MODEL * 2
                   + ATT_HEADS * ATT_GROUP_ROWS * ATT_GROUP_BAND * 4 + 6 * TOKEN_TILE * D_MODEL * 2
                   + (4 * ATT_SCORE_BUFS + 2 * ATT_PROB_BUFS) * 2 * ATT_GROUP_ROWS * ATT_GROUP_BAND)
    return pl.pallas_call(
        _attn_kernel,
        grid=(b, s // TOKEN_TILE),
        in_specs=[
            tile, cur, prev, cur, prev, cur,
            _resident((ATT_PAIRS, ATT_GROUP_BAND, 2 * ATT_GROUP_ROWS)), _resident((D_MODEL, D_MODEL)),
        ],
        out_specs=tile,
        out_shape=jax.ShapeDtypeStruct(x.shape, F32),
        scratch_shapes=[
            pltpu.VMEM((ATT_PAIRS, 2 * TOKEN_TILE, V7X_LANES), BF16),
            pltpu.VMEM((ATT_PAIRS, 2 * TOKEN_TILE, V7X_LANES), BF16),
            pltpu.VMEM((ATT_GROUPS, ATT_PAIRS, V7X_LANES, ATT_GROUP_ROWS), BF16),
        ] + [score_buf] * ATT_SCORE_BUFS + [prob_buf] * ATT_PROB_BUFS,
        compiler_params=_params(2, block_bytes),
        name="attn_band",
    )(x, q, k, k, v, v, bias, w_out)


def kernel(x, mix_norm, ffn_norm, w_ffn_in, w_ffn_out, ab_w_in, ab_gn_gain, ab_w_pool, ab_pool_scale,
           ab_w_out, c_w_qkv, c_rel_bias, c_w_out, final_norm):
    b, s, d = x.shape
    depth = mix_norm.shape[0]
    assert d == D_MODEL and s % max(TOKEN_TILE, MIX_TOKEN_TILE, QKV_TOKEN_TILE, FFN_TOKEN_TILE) == 0, (x.shape,)
    n = b * s

    cos, sin = _rotary_tables(s)
    dmask, qdec, kdec = _decay_tables()
    fg = final_norm.reshape(1, D_MODEL)

    for layer in range(depth):
        i = layer // 2
        g_mix = mix_norm[layer].reshape(1, D_MODEL)
        if layer % 2 == 0:
            x = _mixer(x, g_mix, ab_w_in[i].astype(BF16), cos, sin, dmask, qdec, kdec,
                       ab_gn_gain[i].reshape(1, RET_WIDTH), ab_w_pool[i].astype(BF16),
                       ab_pool_scale[i].reshape(1, POOL_WIDTH), ab_w_out[i].astype(BF16))
        else:
            q, k, v = _qkv(x, g_mix, c_w_qkv[i].astype(BF16))
            bias = _bias_table(c_rel_bias[i]).reshape(ATT_PAIRS, 2 * ATT_GROUP_ROWS, ATT_GROUP_BAND)
            bias = jnp.transpose(bias, (0, 2, 1))
            x = _attn(x, q, k, v, bias, c_w_out[i].astype(BF16))
        x = _ffn(x.reshape(n, d), ffn_norm[layer].reshape(1, D_MODEL), w_ffn_in[layer].astype(BF16),
                 w_ffn_out[layer].astype(BF16), fg, final_norm=(layer == depth - 1)).reshape(b, s, d)
    return x
```
